```python
import math
import jax, jax.numpy as jnp
from jax import lax
import numpy as np

D_MODEL = 1024
BATCH = 32
SEQ = 2048
DEPTH = 2
DEC_BATCH = 128
DEC_SEQ = 4
PAST_LEN = 16384
PAGE_SIZE = 128

MLA_H = 4
MLA_NOPE = 64
MLA_ROPE = 32
MLA_V = 64
MLA_QLORA = 384
MLA_KVLORA = 128
ROPE_BASE = 10000.0
HG_H = 4
HG_DK = 64
HG_DV = 64
HG_CHUNK = 16
NSA_H = 4
NSA_D = 64
NSA_CMP = 32
NSA_SEL = 64
NSA_TOPN = 16
NSA_WINDOW = 512
NSA_QBLK = 64
S5_CH = 16
S5_G = 16
S5_N = 64
D_FF = 2688
CONV_W = 3
ATT_QBLK = 128
EPS = 1e-6
NEG = -1e30
TINY = 1e-30

MLA_COLS = MLA_QLORA + MLA_KVLORA + MLA_ROPE
HG_COLS = 2 * HG_H * HG_DK + 2 * HG_H * HG_DV
NSA_COLS = NSA_H * NSA_D + 6 * NSA_D + 3 * NSA_H
S5_COLS = S5_G * S5_CH
D_IN = MLA_COLS + HG_COLS + NSA_COLS + S5_COLS
SPLITS = (MLA_COLS, MLA_COLS + HG_COLS, MLA_COLS + HG_COLS + NSA_COLS)
D_MIX = MLA_H * MLA_V + HG_H * HG_DV + NSA_H * NSA_D + S5_COLS

kernel_name = "hybrid_mla_hgrn2_nsa_s5_step"


def rmsnorm(x, g):
    xf = x.astype(jnp.float32)
    y = xf * lax.rsqrt(jnp.mean(xf * xf, axis=-1, keepdims=True) + EPS)
    return (y * g.astype(jnp.float32)).astype(x.dtype)


def masked_softmax(s, mask):
    s = jnp.where(mask, s.astype(jnp.float32), NEG)
    m = jnp.max(s, axis=-1, keepdims=True)
    e = jnp.where(mask, jnp.exp(s - m), 0.0)
    return e / jnp.maximum(jnp.sum(e, axis=-1, keepdims=True), TINY)


def rope(x, pos):
    half = x.shape[-1] // 2
    inv = ROPE_BASE ** (-jnp.arange(half, dtype=jnp.float32) / half)
    ang = pos.astype(jnp.float32)[:, None] * inv[None, :]
    shp = (pos.shape[0],) + (1,) * (x.ndim - 3) + (half,)
    cos, sin = jnp.cos(ang).reshape(shp), jnp.sin(ang).reshape(shp)
    xf = x.astype(jnp.float32)
    x1, x2 = xf[..., :half], xf[..., half:]
    return jnp.concatenate([x1 * cos - x2 * sin, x2 * cos + x1 * sin], axis=-1).astype(x.dtype)


def alibi_slopes(n):
    return 2.0 ** (-8.0 * jnp.arange(1, n + 1, dtype=jnp.float32) / n)


def dsl(a, s, n):
    return lax.dynamic_slice_in_dim(a, s, n, axis=1)


def map_query_blocks(fn, n_q, blk):
    nb = n_q // blk
    out = jnp.moveaxis(lax.map(fn, jnp.arange(nb)), 0, 1)
    return out.reshape(out.shape[0], nb * blk, *out.shape[3:])


def gather_pages(pool, l, page_table):
    g = pool[l, page_table]
    return g.reshape(g.shape[0], g.shape[1] * g.shape[2], *pool.shape[3:])


def mla_attend(q_lat, q_rope, ckv, kr, q_pos, k_pos):
    s = jnp.einsum('bqhr,bkr->bhqk', q_lat, ckv) + jnp.einsum('bqhd,bkd->bhqk', q_rope, kr)
    s = s * (MLA_NOPE + MLA_ROPE) ** -0.5
    p = masked_softmax(s, (k_pos[None, :] <= q_pos[:, None])[None, None])
    return jnp.einsum('bhqk,bkr->bqhr', p.astype(ckv.dtype), ckv)


def gla_chunked(q, k, v, logf, S0):
    B, L, H, DK = q.shape
    C = min(HG_CHUNK, L)
    pad = (-L) % C
    padw = ((0, 0), (0, pad), (0, 0), (0, 0))
    q, k, v, logf = (jnp.pad(a, padw) for a in (q, k, v, logf))
    n = (L + pad) // C
    chunks = lambda a: jnp.moveaxis(a.reshape(B, n, C, *a.shape[2:]), 1, 0)
    causal = (jnp.arange(C)[:, None] >= jnp.arange(C)[None, :])[None, :, :, None, None]

    def step(S, inp):
        qc, kc, vc, lc = inp
        b = jnp.cumsum(lc, axis=1)
        decay = jnp.exp(jnp.where(causal, b[:, :, None] - b[:, None, :], -jnp.inf))
        A = jnp.einsum('bthd,bshd,btshd->bhts', qc, kc, decay)
        o = jnp.einsum('bhts,bshv->bthv', A, vc) + jnp.einsum('bthd,bhdv->bthv', qc * jnp.exp(b), S)
        bl = b[:, -1]
        S_new = jnp.exp(bl)[..., None] * S + jnp.einsum('bshd,bshv->bhdv', kc * jnp.exp(bl[:, None] - b), vc)
        return S_new, o

    S, o = lax.scan(step, S0, (chunks(q), chunks(k), chunks(v), chunks(logf)))
    o = jnp.moveaxis(o, 0, 1).reshape(B, n * C, H, v.shape[-1])[:, :L]
    return o, S


def s5_mixer(u, l, W, x0_re, x0_im):
    B, L, _ = u.shape
    f32 = jnp.float32
    ug = u.reshape(B, L, S5_G, S5_CH).astype(f32)
    ar, ai = W['s5_a_re'][l].astype(f32), W['s5_a_im'][l].astype(f32)
    dt = jnp.exp(W['s5_log_dt'][l].astype(f32))[:, None]
    mag = jnp.exp(dt * ar)
    abr, abi = mag * jnp.cos(dt * ai), mag * jnp.sin(dt * ai)
    den = ar * ar + ai * ai
    fr = ((abr - 1.0) * ar + abi * ai) / den
    fi = (abi * ar - (abr - 1.0) * ai) / den
    br, bi = W['s5_b_re'][l].astype(f32), W['s5_b_im'][l].astype(f32)
    bbr = fr[..., None] * br - fi[..., None] * bi
    bbi = fr[..., None] * bi + fi[..., None] * br
    bur = jnp.einsum('gnc,blgc->blgn', bbr, ug)
    bui = jnp.einsum('gnc,blgc->blgn', bbi, ug)
    x0r, x0i = x0_re.astype(f32), x0_im.astype(f32)
    bur = bur.at[:, 0].add(abr * x0r - abi * x0i)
    bui = bui.at[:, 0].add(abr * x0i + abi * x0r)

    def comb(e1, e2):
        a1r, a1i, b1r, b1i = e1
        a2r, a2i, b2r, b2i = e2
        return (a2r * a1r - a2i * a1i, a2r * a1i + a2i * a1r,
                a2r * b1r - a2i * b1i + b2r, a2r * b1i + a2i * b1r + b2i)

    _, _, xr, xi = lax.associative_scan(
        comb, (jnp.broadcast_to(abr, bur.shape), jnp.broadcast_to(abi, bui.shape), bur, bui), axis=1)
    y = (jnp.einsum('gcn,blgn->blgc', W['s5_c_re'][l].astype(f32), xr)
         - jnp.einsum('gcn,blgn->blgc', W['s5_c_im'][l].astype(f32), xi)
         + W['s5_d'][l].astype(f32) * ug)
    y = jax.nn.gelu(y.reshape(B, L, S5_COLS)).astype(u.dtype)
    y = y * jax.nn.sigmoid(y @ W['s5_w_glu'][l] + W['s5_b_glu'][l])
    return y, xr[:, -1].astype(u.dtype), xi[:, -1].astype(u.dtype)


def nsa_blocks(rows):
    B, Lk = rows.shape[:2]
    r = jnp.pad(rows, ((0, 0), (0, (-Lk) % NSA_SEL), (0, 0), (0, 0)))
    nS = r.shape[1] // NSA_SEL
    cm = r[:, :, :2].astype(jnp.float32).reshape(B, nS * (NSA_SEL // NSA_CMP), NSA_CMP, 2, NSA_D)
    cm = jnp.mean(cm, axis=2).astype(rows.dtype)
    sb = r[:, :, 2:].reshape(B, nS, NSA_SEL, 2, NSA_D)
    return cm[:, :, 0], cm[:, :, 1], sb[:, :, :, 0], sb[:, :, :, 1]


def nsa_core(q, q_pos, gates, kcm, vcm, ksb, vsb, kw, vw, kw_pos):
    B, Lq, H, D = q.shape
    f32 = jnp.float32
    scale = D ** -0.5
    slopes = alibi_slopes(H)[None, None, :, None]
    qp = q_pos[:, None]
    nC = kcm.shape[1]
    c_end = jnp.arange(nC) * NSA_CMP + NSA_CMP - 1
    dc = (qp - c_end[None, :])
    s_c = jnp.einsum('bqhd,bjd->bqhj', q, kcm) * scale - slopes * dc.astype(f32)[None, :, None, :]
    p_c = masked_softmax(s_c, (dc >= 0)[None, :, None, :])
    o_c = jnp.einsum('bqhj,bjd->bqhd', p_c.astype(vcm.dtype), vcm)
    nS = ksb.shape[1]
    imp = jnp.sum(p_c, axis=2).reshape(B, Lq, nS, NSA_SEL // NSA_CMP).sum(-1)
    blk = jnp.arange(nS)
    imp = jnp.where(blk[None, :] == (q_pos // NSA_SEL)[:, None], jnp.inf,
                    jnp.where(blk[None, :] * NSA_SEL <= qp, imp, -jnp.inf))
    n_sel = min(NSA_TOPN, nS)
    _, idx = lax.top_k(imp, n_sel)
    bi = jnp.arange(B)[:, None, None]
    k_sel = ksb[bi, idx].reshape(B, Lq, n_sel * NSA_SEL, D)
    v_sel = vsb[bi, idx].reshape(B, Lq, n_sel * NSA_SEL, D)
    k_pos = (idx[..., None] * NSA_SEL + jnp.arange(NSA_SEL)).reshape(B, Lq, n_sel * NSA_SEL)
    ds = q_pos[None, :, None] - k_pos
    s_s = jnp.einsum('bqhd,bqkd->bqhk', q, k_sel) * scale - slopes * ds.astype(f32)[:, :, None, :]
    p_s = masked_softmax(s_s, (ds >= 0)[:, :, None, :])
    o_s = jnp.einsum('bqhk,bqkd->bqhd', p_s.astype(v_sel.dtype), v_sel)
    dw = qp - kw_pos[None, :]
    s_w = jnp.einsum('bqhd,bkd->bqhk', q, kw) * scale - slopes * dw.astype(f32)[None, :, None, :]
    mw = (dw >= 0) & (dw < NSA_WINDOW) & (kw_pos[None, :] >= 0)
    p_w = masked_softmax(s_w, mw[None, :, None, :])
    o_w = jnp.einsum('bqhk,bkd->bqhd', p_w.astype(vw.dtype), vw)
    return gates[..., 0:1] * o_c + gates[..., 1:2] * o_s + gates[..., 2:3] * o_w


def layer(x, c, l, W, q_pos, past):
    B, L, _ = x.shape
    f32 = jnp.float32
    mod = (jax.nn.silu(c) @ W['w_ada'][l] + W['b_ada'][l]).reshape(B, 6, 1, D_MODEL)
    sh1, sc1, g1, sh2, sc2, g2 = (mod[:, i] for i in range(6))
    h = rmsnorm(x, W['g_mix'][l]) * (1 + sc1) + sh1
    cols = h @ W['w_in'][l]
    a_c, b_c, c_c, d_c = jnp.split(cols, SPLITS, axis=-1)

    cq, ckv_raw, kr_raw = jnp.split(a_c, [MLA_QLORA, MLA_QLORA + MLA_KVLORA], axis=-1)
    q = (rmsnorm(cq, W['mla_g_q'][l]) @ W['mla_w_uq'][l]).reshape(B, L, MLA_H, MLA_NOPE + MLA_ROPE)
    q_lat = jnp.einsum('bqhn,rhn->bqhr', q[..., :MLA_NOPE], W['mla_w_uk'][l])
    q_rope = rope(q[..., MLA_NOPE:], q_pos)
    mla_rows = jnp.concatenate([rmsnorm(ckv_raw, W['mla_g_kv'][l]), rope(kr_raw, q_pos)], axis=-1)
    keys = mla_rows if past is None else jnp.concatenate(
        [gather_pages(past['mla'], l, past['pt']), mla_rows], axis=1)
    k_pos = jnp.arange(keys.shape[1])
    ckv_k, kr_k = keys[..., :MLA_KVLORA], keys[..., MLA_KVLORA:]
    if L > ATT_QBLK and L % ATT_QBLK == 0:
        def mla_blk(i):
            s = i * ATT_QBLK
            return mla_attend(dsl(q_lat, s, ATT_QBLK), dsl(q_rope, s, ATT_QBLK), ckv_k, kr_k,
                              lax.dynamic_slice_in_dim(q_pos, s, ATT_QBLK), k_pos)
        o_lat = map_query_blocks(mla_blk, L, ATT_QBLK)
    else:
        o_lat = mla_attend(q_lat, q_rope, ckv_k, kr_k, q_pos, k_pos)
    o_mla = jnp.einsum('bqhr,rhv->bqhv', o_lat, W['mla_w_uv'][l]).reshape(B, L, MLA_H * MLA_V)

    nk, nv = HG_H * HG_DK, HG_H * HG_DV
    hq, hf, hi, hg = jnp.split(b_c, [nk, 2 * nk, 2 * nk + nv], axis=-1)
    lbs = jax.nn.softmax(W['hg_lb'].astype(f32), axis=0)
    lb = (jnp.cumsum(lbs, axis=0) - lbs[0])[l]
    z = hf.astype(f32)
    logf = jnp.log(lb + (1.0 - lb) * jax.nn.sigmoid(z)).reshape(B, L, HG_H, HG_DK)
    kg = ((1.0 - lb) * jax.nn.sigmoid(-z)).reshape(B, L, HG_H, HG_DK)
    qg = jax.nn.silu(hq.astype(f32)).reshape(B, L, HG_H, HG_DK)
    vg = hi.astype(f32).reshape(B, L, HG_H, HG_DV)
    S0 = jnp.zeros((B, HG_H, HG_DK, HG_DV), f32) if past is None else past['hgrn'][l].astype(f32)
    o_hg, S_hg = gla_chunked(qg, kg, vg, logf, S0)
    o_hg = (rmsnorm(o_hg.astype(x.dtype), W['hg_g_out'][l]).reshape(B, L, nv) * jax.nn.silu(hg))

    qn = c_c[..., :NSA_H * NSA_D].reshape(B, L, NSA_H, NSA_D)
    kvs = c_c[..., NSA_H * NSA_D:NSA_H * NSA_D + 6 * NSA_D].reshape(B, L, 6, NSA_D)
    gates = jax.nn.sigmoid(c_c[..., NSA_H * NSA_D + 6 * NSA_D:]).reshape(B, L, NSA_H, 3)
    nsa_rows, win_new = kvs[:, :, :4], kvs[:, :, 4:]
    if past is None:
        kcm, vcm, ksb, vsb = nsa_blocks(nsa_rows)
        wpad = jnp.pad(win_new, ((0, 0), (NSA_WINDOW, 0), (0, 0), (0, 0)))
        qb = NSA_QBLK if L % NSA_QBLK == 0 else L

        def nsa_blk(i):
            s = i * qb
            wb = dsl(wpad, s, NSA_WINDOW + qb)
            return nsa_core(dsl(qn, s, qb), s + jnp.arange(qb), dsl(gates, s, qb), kcm, vcm, ksb, vsb,
                            wb[:, :, 0], wb[:, :, 1], s - NSA_WINDOW + jnp.arange(NSA_WINDOW + qb))
        o_nsa = map_query_blocks(nsa_blk, L, qb)
        win_state = win_new[:, -min(NSA_WINDOW, L):]
    else:
        rows_all = jnp.concatenate([gather_pages(past['nsa'], l, past['pt']), nsa_rows], axis=1)
        kcm, vcm, ksb, vsb = nsa_blocks(rows_all)
        wk = jnp.concatenate([past['win'][l], win_new], axis=1)
        wbl = past['win'].shape[2]
        o_nsa = nsa_core(qn, q_pos, gates, kcm, vcm, ksb, vsb, wk[:, :, 0], wk[:, :, 1],
                         q_pos[0] - wbl + jnp.arange(wbl + L))
        win_state = wk[:, -wbl:]
    o_nsa = o_nsa.reshape(B, L, NSA_H * NSA_D).astype(x.dtype)

    if past is None:
        z0 = jnp.zeros((B, S5_G, S5_N), f32)
        o_s5, s5r, s5i = s5_mixer(d_c, l, W, z0, z0)
    else:
        o_s5, s5r, s5i = s5_mixer(d_c, l, W, past['s5_re'][l], past['s5_im'][l])

    mix = jnp.concatenate([o_mla, o_hg.astype(x.dtype), o_nsa, o_s5], axis=-1) @ W['w_out'][l]
    x = x + g1 * mix

    h2 = rmsnorm(x, W['g_ffn'][l]) * (1 + sc2) + sh2
    up = h2 @ W['ffn_w_up'][l]
    buf = jnp.zeros((B, CONV_W - 1, 2 * D_FF), up.dtype) if past is None else past['conv'][l].astype(up.dtype)
    ext = jnp.concatenate([buf, up], axis=1)
    cw = W['ffn_conv_w'][l]
    conv = sum(ext[:, j:j + L] * cw[j] for j in range(CONV_W)) + W['ffn_conv_b'][l]
    gate, val = jnp.split(conv, 2, axis=-1)
    x = x + g2 * ((jax.nn.silu(gate) * val) @ W['ffn_w_down'][l])

    state = dict(mla=mla_rows, nsa=nsa_rows, win=win_state, hgrn=S_hg.astype(x.dtype),
                 s5_re=s5r, s5_im=s5i, conv=ext[:, -(CONV_W - 1):])
    return x, state


def setup_inputs(seed: int = 0) -> dict:
    key = jax.random.key(seed)
    ks = iter(jax.random.split(key, 64))
    f32 = jnp.float32
    nrm = lambda shape, s: jax.random.normal(next(ks), shape, f32) * s
    n_pages = PAST_LEN // PAGE_SIZE
    n_phys = (DEC_BATCH * n_pages * 5 + 3) // 4
    wb = min(NSA_WINDOW, PAST_LEN)
    perm = jax.random.permutation(next(ks), n_phys)[:DEC_BATCH * n_pages]
    page_table = perm.reshape(DEC_BATCH, n_pages).astype(jnp.int32)
    a_im = jnp.broadcast_to(math.pi * jnp.arange(S5_N, dtype=f32), (DEPTH, S5_G, S5_N)) + nrm((DEPTH, S5_G, S5_N), 0.01)
    log_dt = jax.random.uniform(next(ks), (DEPTH, S5_G), f32, math.log(0.001), math.log(0.1))
    return {
        'x_prompt': nrm((BATCH, SEQ, D_MODEL), 1.0),
        'x_sample': nrm((DEC_BATCH, DEC_SEQ, D_MODEL), 1.0),
        'c_prompt': nrm((BATCH, D_MODEL), 1.0),
        'c_sample': nrm((DEC_BATCH, D_MODEL), 1.0),
        'cache_mla': nrm((DEPTH, n_phys, PAGE_SIZE, MLA_KVLORA + MLA_ROPE), 1.0),
        'cache_nsa': nrm((DEPTH, n_phys, PAGE_SIZE, 4, NSA_D), 1.0),
        'cache_nsa_win': nrm((DEPTH, DEC_BATCH, wb, 2, NSA_D), 1.0),
        'state_hgrn': nrm((DEPTH, DEC_BATCH, HG_H, HG_DK, HG_DV), 0.5),
        'state_s5_re': nrm((DEPTH, DEC_BATCH, S5_G, S5_N), 0.3),
        'state_s5_im': nrm((DEPTH, DEC_BATCH, S5_G, S5_N), 0.3),
        'state_ffn_conv': nrm((DEPTH, DEC_BATCH, CONV_W - 1, 2 * D_FF), 1.0),
        'page_table': page_table,
        'w_ada': nrm((DEPTH, D_MODEL, 6 * D_MODEL), D_MODEL ** -0.5),
        'b_ada': nrm((DEPTH, 6 * D_MODEL), 0.01),
        'g_mix': 1.0 + nrm((DEPTH, D_MODEL), 0.01),
        'g_ffn': 1.0 + nrm((DEPTH, D_MODEL), 0.01),
        'w_in': nrm((DEPTH, D_MODEL, D_IN), D_MODEL ** -0.5),
        'w_out': nrm((DEPTH, D_MIX, D_MODEL), D_MIX ** -0.5),
        'mla_g_q': 1.0 + nrm((DEPTH, MLA_QLORA), 0.01),
        'mla_g_kv': 1.0 + nrm((DEPTH, MLA_KVLORA), 0.01),
        'mla_w_uq': nrm((DEPTH, MLA_QLORA, MLA_H * (MLA_NOPE + MLA_ROPE)), MLA_QLORA ** -0.5),
        'mla_w_uk': nrm((DEPTH, MLA_KVLORA, MLA_H, MLA_NOPE), MLA_KVLORA ** -0.5),
        'mla_w_uv': nrm((DEPTH, MLA_KVLORA, MLA_H, MLA_V), MLA_KVLORA ** -0.5),
        'hg_lb': nrm((DEPTH, HG_H * HG_DK), 0.5),
        'hg_g_out': 1.0 + nrm((DEPTH, HG_DV), 0.01),
        's5_a_re': -0.5 + nrm((DEPTH, S5_G, S5_N), 0.01),
        's5_a_im': a_im,
        's5_log_dt': log_dt,
        's5_b_re': nrm((DEPTH, S5_G, S5_N, S5_CH), S5_CH ** -0.5),
        's5_b_im': nrm((DEPTH, S5_G, S5_N, S5_CH), S5_CH ** -0.5),
        's5_c_re': nrm((DEPTH, S5_G, S5_CH, S5_N), 1.0),
        's5_c_im': nrm((DEPTH, S5_G, S5_CH, S5_N), 1.0),
        's5_d': nrm((DEPTH, S5_G, S5_CH), 0.5),
        's5_w_glu': nrm((DEPTH, S5_COLS, S5_COLS), S5_COLS ** -0.5),
        's5_b_glu': nrm((DEPTH, S5_COLS), 0.01),
        'ffn_w_up': nrm((DEPTH, D_MODEL, 2 * D_FF), D_MODEL ** -0.5),
        'ffn_conv_w': nrm((DEPTH, CONV_W, 2 * D_FF), CONV_W ** -0.5),
        'ffn_conv_b': nrm((DEPTH, 2 * D_FF), 0.01),
        'ffn_w_down': nrm((DEPTH, D_FF, D_MODEL), D_FF ** -0.5),
        'g_final': 1.0 + nrm((D_MODEL,), 0.01),
    }


def reference(x_prompt, x_sample, c_prompt, c_sample, cache_mla, cache_nsa, cache_nsa_win, state_hgrn,
              state_s5_re, state_s5_im, state_ffn_conv, page_table, w_ada, b_ada, g_mix, g_ffn, w_in, w_out,
              mla_g_q, mla_g_kv, mla_w_uq, mla_w_uk, mla_w_uv, hg_lb, hg_g_out, s5_a_re, s5_a_im, s5_log_dt,
              s5_b_re, s5_b_im, s5_c_re, s5_c_im, s5_d, s5_w_glu, s5_b_glu, ffn_w_up, ffn_conv_w, ffn_conv_b,
              ffn_w_down, g_final):
    W = dict(w_ada=w_ada, b_ada=b_ada, g_mix=g_mix, g_ffn=g_ffn, w_in=w_in, w_out=w_out,
             mla_g_q=mla_g_q, mla_g_kv=mla_g_kv, mla_w_uq=mla_w_uq, mla_w_uk=mla_w_uk, mla_w_uv=mla_w_uv,
             hg_lb=hg_lb, hg_g_out=hg_g_out, s5_a_re=s5_a_re, s5_a_im=s5_a_im, s5_log_dt=s5_log_dt,
             s5_b_re=s5_b_re, s5_b_im=s5_b_im, s5_c_re=s5_c_re, s5_c_im=s5_c_im, s5_d=s5_d,
             s5_w_glu=s5_w_glu, s5_b_glu=s5_b_glu, ffn_w_up=ffn_w_up, ffn_conv_w=ffn_conv_w,
             ffn_conv_b=ffn_conv_b, ffn_w_down=ffn_w_down)
    past_len = page_table.shape[1] * cache_mla.shape[2]
    pos_p = jnp.arange(x_prompt.shape[1])
    pos_s = past_len + jnp.arange(x_sample.shape[1])
    past = dict(pt=page_table, mla=cache_mla, nsa=cache_nsa, win=cache_nsa_win, hgrn=state_hgrn,
                s5_re=state_s5_re, s5_im=state_s5_im, conv=state_ffn_conv)
    xp, xs = x_prompt, x_sample
    sp, ss = [], []
    for l in range(DEPTH):
        xp, st_p = layer(xp, c_prompt, l, W, pos_p, None)
        xs, st_s = layer(xs, c_sample, l, W, pos_s, past)
        sp.append(st_p)
        ss.append(st_s)
    stk = lambda lst, name: jnp.stack([d[name] for d in lst], axis=0)
    y_prompt = rmsnorm(xp, g_final)
    y_sample = rmsnorm(xs, g_final)
    return (y_prompt, y_sample,
            stk(sp, 'mla'), stk(ss, 'mla'),
            stk(sp, 'nsa'), stk(ss, 'nsa'),
            stk(sp, 'win'), stk(ss, 'win'),
            stk(sp, 'hgrn'), stk(ss, 'hgrn'),
            stk(sp, 's5_re'), stk(ss, 's5_re'),
            stk(sp, 's5_im'), stk(ss, 's5_im'),
            stk(sp, 'conv'), stk(ss, 'conv'))
```

```python
import functools
import math

import jax
import jax.numpy as jnp
from jax import lax
from jax.experimental import pallas as pl
from jax.experimental.pallas import tpu as pltpu

F32 = jnp.float32
BF16 = jnp.bfloat16

MLA_H, MLA_NOPE, MLA_ROPE, MLA_V = 4, 64, 32, 64
MLA_QLORA, MLA_KVLORA = 384, 128
ROPE_BASE = 10000.0
HG_H, HG_DK, HG_DV = 4, 64, 64
NSA_H, NSA_D, NSA_CMP, NSA_SEL, NSA_TOPN, NSA_WINDOW = 4, 64, 32, 64, 16, 512
S5_CH, S5_G, S5_N = 16, 16, 64
CONV_W = 3
EPS = 1e-6
NEG = -1e30
TINY = 1e-30

MLA_COLS = MLA_QLORA + MLA_KVLORA + MLA_ROPE
HG_COLS = 2 * HG_H * HG_DK + 2 * HG_H * HG_DV
NSA_COLS = NSA_H * NSA_D + 6 * NSA_D + 3 * NSA_H
S5_COLS = S5_G * S5_CH
MLA_ROW = MLA_KVLORA + MLA_ROPE
S5_STATE = S5_G * S5_N

V7X_LANES = 128
V7X_SUBLANES = 8
V7X_VMEM_LIMIT_BYTES = 56 * 1024 * 1024

PC_CQ = 0
PC_CKV = 384
PC_MISC = 512
PC_GATE_LANE = 64
PC_HG = 640
PC_QN = 1664
PC_NSA = 1920
PC_WIN = 2176
PC_S5 = 2304
PC_TOTAL = 2560


def _cparams(*semantics):
    return pltpu.CompilerParams(dimension_semantics=tuple(semantics), vmem_limit_bytes=V7X_VMEM_LIMIT_BYTES)


def _dot(a, b):
    return jnp.dot(a, b, preferred_element_type=F32)


def _dot_nt(a, b):
    return lax.dot_general(a, b, (((1,), (1,)), ((), ())), preferred_element_type=F32)


def _dot_tn(a, b):
    return lax.dot_general(a, b, (((0,), (0,)), ((), ())), preferred_element_type=F32)


def _dot_split(a, b_bf16):
    hi = a.astype(BF16)
    lo = (a - hi.astype(F32)).astype(BF16)
    return _dot(hi, b_bf16) + _dot(lo, b_bf16)


def _silu(x):
    return x * jax.nn.sigmoid(x)


def _mod_kernel(c_ref, w_ref, b_ref, o_ref):
    s = _silu(c_ref[...]).astype(BF16)
    o_ref[...] = _dot(s, w_ref[...].astype(BF16)) + b_ref[...]


def _modulation(c_all, w_ada, b_ada):
    depth, d, n = w_ada.shape
    bc = c_all.shape[0]
    tn = 1536
    return pl.pallas_call(
        _mod_kernel,
        out_shape=jax.ShapeDtypeStruct((depth, bc, n), F32),
        grid=(depth, n // tn),
        in_specs=[pl.BlockSpec((bc, d), lambda l, j: (0, 0)),
                  pl.BlockSpec((None, d, tn), lambda l, j: (l, 0, j)),
                  pl.BlockSpec((None, 1, tn), lambda l, j: (l, 0, j))],
        out_specs=pl.BlockSpec((None, bc, tn), lambda l, j: (l, 0, j)),
        compiler_params=_cparams("parallel", "parallel"),
        name="adaln_modulation",
    )(c_all, w_ada, b_ada.reshape(depth, 1, n))


def _proj_in_kernel(x_ref, sc_ref, sh_ref, g_ref, w_ref, gkv_ref, cos_ref, sin_ref,
                    cq_ref, mla_ref, hg_ref, qn_ref, nsa_ref, win_ref, gate_ref, s5_ref):
    x = x_ref[...]
    var = jnp.mean(x * x, axis=-1, keepdims=True)
    h = (x * lax.rsqrt(var + EPS) * g_ref[...]) * (1.0 + sc_ref[...]) + sh_ref[...]
    cols = _dot(h.astype(BF16), w_ref[...])
    cq_ref[...] = cols[:, PC_CQ:PC_CQ + MLA_QLORA]
    ckv = cols[:, PC_CKV:PC_CKV + MLA_KVLORA]
    kvar = jnp.mean(ckv * ckv, axis=-1, keepdims=True)
    mla_ref[:, 0:MLA_KVLORA] = ckv * lax.rsqrt(kvar + EPS) * gkv_ref[...]
    kr = cols[:, PC_MISC:PC_MISC + MLA_ROPE]
    kr_swapped = cols[:, PC_MISC + MLA_ROPE:PC_MISC + 2 * MLA_ROPE]
    mla_ref[:, MLA_KVLORA:MLA_ROW] = kr * cos_ref[...] + kr_swapped * sin_ref[...]
    gate_ref[...] = jax.nn.sigmoid(cols[:, PC_MISC:PC_MISC + V7X_LANES])
    hg_ref[...] = cols[:, PC_HG:PC_HG + HG_COLS]
    qn_ref[...] = cols[:, PC_QN:PC_QN + NSA_H * NSA_D] * (NSA_D ** -0.5)
    nsa_ref[...] = cols[:, PC_NSA:PC_NSA + 4 * NSA_D]
    win_ref[...] = cols[:, PC_WIN:PC_WIN + 2 * NSA_D]
    s5_ref[...] = cols[:, PC_S5:PC_S5 + S5_COLS]


def _proj_in(x2d, scale, shift, g_mix, w_perm, g_kv, cos_t, sin_t, rows_per_seq, tm):
    r, d = x2d.shape
    if rows_per_seq is not None:
        tps = rows_per_seq // tm
        mod_spec = pl.BlockSpec((None, 1, d), lambda i: (i // tps, 0, 0))
        rope_spec = pl.BlockSpec((tm, MLA_ROPE), lambda i: (i % tps, 0))
    else:
        mod_spec = pl.BlockSpec((tm, d), lambda i: (i, 0))
        rope_spec = pl.BlockSpec((tm, MLA_ROPE), lambda i: (i, 0))
    row = lambda n: pl.BlockSpec((tm, n), lambda i: (i, 0))
    const = lambda a, b: pl.BlockSpec((a, b), lambda i: (0, 0))
    widths = (MLA_QLORA, MLA_ROW, HG_COLS, NSA_H * NSA_D, 4 * NSA_D, 2 * NSA_D, V7X_LANES, S5_COLS)
    return pl.pallas_call(
        _proj_in_kernel,
        out_shape=[jax.ShapeDtypeStruct((r, n), F32) for n in widths],
        grid=(r // tm,),
        in_specs=[row(d), mod_spec, mod_spec, const(1, d), const(d, PC_TOTAL), const(1, MLA_KVLORA),
                  rope_spec, rope_spec],
        out_specs=[row(n) for n in widths],
        compiler_params=_cparams("parallel"),
        name="proj_in",
    )(x2d, scale, shift, g_mix, w_perm, g_kv, cos_t, sin_t)


def _mla_q_kernel(cq_ref, g_ref, wuq_ref, wuk_ref, cos_ref, sin_ref, qlat_ref, qrope_ref):
    cq = cq_ref[...]
    var = jnp.mean(cq * cq, axis=-1, keepdims=True)
    c = cq * lax.rsqrt(var + EPS) * g_ref[...]
    q = _dot(c.astype(BF16), wuq_ref[...])
    scale = (MLA_NOPE + MLA_ROPE) ** -0.5
    n_nope = MLA_H * MLA_NOPE
    n_rope = MLA_H * MLA_ROPE
    qlat_ref[...] = _dot(q[:, :n_nope].astype(BF16), wuk_ref[...]) * scale
    qrope_ref[...] = (q[:, n_nope:n_nope + n_rope] * cos_ref[...]
                      + q[:, n_nope + n_rope:] * sin_ref[...]) * scale


def _mla_q(cq2d, g_q, wuq_perm, wuk_bd, cos_t, sin_t, rows_per_seq, tm):
    r = cq2d.shape[0]
    n_rope = MLA_H * MLA_ROPE
    if rows_per_seq is not None:
        tps = rows_per_seq // tm
        rope_spec = pl.BlockSpec((tm, n_rope), lambda i: (i % tps, 0))
    else:
        rope_spec = pl.BlockSpec((tm, n_rope), lambda i: (i, 0))
    row = lambda n: pl.BlockSpec((tm, n), lambda i: (i, 0))
    const = lambda a, b: pl.BlockSpec((a, b), lambda i: (0, 0))
    n_lat = MLA_H * MLA_KVLORA
    return pl.pallas_call(
        _mla_q_kernel,
        out_shape=[jax.ShapeDtypeStruct((r, n_lat), F32), jax.ShapeDtypeStruct((r, n_rope), F32)],
        grid=(r // tm,),
        in_specs=[row(MLA_QLORA), const(1, MLA_QLORA), const(MLA_QLORA, wuq_perm.shape[1]),
                  const(MLA_H * MLA_NOPE, n_lat), rope_spec, rope_spec],
        out_specs=[row(n_lat), row(n_rope)],
        compiler_params=_cparams("parallel"),
        name="mla_q_prep",
    )(cq2d, g_q, wuq_perm, wuk_bd, cos_t, sin_t)


def _softmax_step(s, mask, m, l, acc, v_bf16):
    s = jnp.where(mask, s, NEG)
    m_new = jnp.maximum(m, jnp.max(s, axis=-1, keepdims=True))
    alpha = jnp.exp(m - m_new)
    p = jnp.where(mask, jnp.exp(s - m_new), 0.0)
    l_new = alpha * l + jnp.sum(p, axis=-1, keepdims=True)
    acc_new = alpha * acc + _dot(p.astype(BF16), v_bf16)
    return m_new, l_new, acc_new


def _stack_mla_q(qlat, qrope):
    parts = []
    for h in range(MLA_H):
        parts.append(jnp.concatenate([qlat[:, h * MLA_KVLORA:(h + 1) * MLA_KVLORA],
                                      qrope[:, h * MLA_ROPE:(h + 1) * MLA_ROPE]], axis=-1))
    return jnp.concatenate(parts, axis=0).astype(BF16)


def _mla_prompt_kernel(qlat_ref, qrope_ref, keys_ref, o_ref, *, tq, tk):
    i = pl.program_id(1)
    q = _stack_mla_q(qlat_ref[...], qrope_ref[...])
    rows = MLA_H * tq
    qpos = i * tq + lax.rem(lax.broadcasted_iota(jnp.int32, (rows, tk), 0), tq)
    kofs = lax.broadcasted_iota(jnp.int32, (rows, tk), 1)

    def body(c, carry):
        m, l, acc = carry
        start = pl.multiple_of(c * tk, tk)
        kblk = keys_ref[pl.ds(start, tk), :].astype(BF16)
        s = _dot_nt(q, kblk)
        mask = (kofs + c * tk) <= qpos
        return _softmax_step(s, mask, m, l, acc, kblk[:, :MLA_KVLORA])

    n_chunks = (i * tq + tq + tk - 1) // tk
    init = (jnp.full((rows, 1), NEG, F32), jnp.zeros((rows, 1), F32), jnp.zeros((rows, MLA_KVLORA), F32))
    m, l, acc = lax.fori_loop(0, n_chunks, body, init)
    o = acc / jnp.maximum(l, TINY)
    o_ref[...] = jnp.concatenate([o[h * tq:(h + 1) * tq] for h in range(MLA_H)], axis=-1)


def _mla_prompt(qlat, qrope, keys, tq, tk):
    b, l, _ = qlat.shape
    n_lat = MLA_H * MLA_KVLORA
    return pl.pallas_call(
        functools.partial(_mla_prompt_kernel, tq=tq, tk=tk),
        out_shape=jax.ShapeDtypeStruct((b, l, n_lat), F32),
        grid=(b, l // tq),
        in_specs=[pl.BlockSpec((None, tq, n_lat), lambda bi, i: (bi, i, 0)),
                  pl.BlockSpec((None, tq, MLA_H * MLA_ROPE), lambda bi, i: (bi, i, 0)),
                  pl.BlockSpec((None, l, MLA_ROW), lambda bi, i: (bi, 0, 0))],
        out_specs=pl.BlockSpec((None, tq, n_lat), lambda bi, i: (bi, i, 0)),
        compiler_params=_cparams("parallel", "parallel"),
        name="mla_prompt_attention",
    )(qlat, qrope, keys)


def _hgrn_kernel(hg_ref, lb_ref, gout_ref, ones_ref, s0_ref, o_ref, st_ref,
                 st_scr, qe_scr, ke_scr, v_scr, dl_scr, o_scr, *, bt, tl, c, l_valid, layer):
    j = pl.program_id(1)
    nch = tl // c
    rows = bt * tl
    g = rows // c
    width = HG_H * HG_DK

    @pl.when(j == 0)
    def _():
        st_scr[...] = s0_ref[...]

    x = hg_ref[...].reshape(rows, HG_COLS)
    hq = x[:, 0:width]
    z = x[:, width:2 * width]
    vi = x[:, 2 * width:3 * width]
    hgate = x[:, 3 * width:4 * width]
    raw = lb_ref[...]
    e = jnp.exp(raw - jnp.max(raw, axis=0, keepdims=True))
    lbs = e / jnp.sum(e, axis=0, keepdims=True)
    lb = jnp.zeros((1, width), F32)
    for dd in range(1, layer + 1):
        lb = lb + lbs[dd:dd + 1, :]
    t_in =lax.rem(lax.broadcasted_iota(jnp.int32, (rows, width), 0), c)
    valid = (j * tl + lax.rem(lax.broadcasted_iota(jnp.int32, (rows, width), 0), tl)) < l_valid
    logf = jnp.where(valid, jnp.log(lb + (1.0 - lb) * jax.nn.sigmoid(z)), 0.0)
    kg = jnp.where(valid, (1.0 - lb) * jax.nn.sigmoid(-z), 0.0)
    qg = _silu(hq)
    b = logf
    k = 1
    while k < c:
        b = b + jnp.where(t_in >= k, pltpu.roll(b, k, 0), 0.0)
        k *= 2
    b3 = b.reshape(g, c, width)
    q3 = qg.reshape(g, c, width)
    k3 = kg.reshape(g, c, width)
    v3 = vi.reshape(g, c, width)
    ones_bd = ones_ref[...]
    tt = lax.broadcasted_iota(jnp.int32, (g, c, width), 1)
    o = jnp.zeros((g, c, width), F32)
    for s in range(c):
        bs = b3[:, s:s + 1, :]
        w = q3 * k3[:, s:s + 1, :] * jnp.exp(jnp.where(tt >= s, b3 - bs, NEG))
        r = _dot(w.reshape(rows, width).astype(BF16), ones_bd)
        o = o + r.reshape(g, c, width) * v3[:, s:s + 1, :]
    bl = b3[:, c - 1:c, :]
    o_scr[...] = o.reshape(bt, nch, c, width)
    qe_scr[...] = (q3 * jnp.exp(b3)).reshape(bt, nch, c, width).astype(BF16)
    ke_scr[...] = (k3 * jnp.exp(bl - b3)).reshape(bt, nch, c, width).astype(BF16)
    v_scr[...] = v3.reshape(bt, nch, c, width).astype(BF16)
    dl_scr[...] = jnp.exp(bl).reshape(bt, nch, 1, width)
    bd_mask = ones_bd.astype(F32)

    def chunk(n, carry):
        for bi in range(bt):
            st = st_scr[bi]
            o_scr[bi, n] = o_scr[bi, n] + _dot_nt(qe_scr[bi, n], st.astype(BF16))
            upd = _dot_tn(v_scr[bi, n], ke_scr[bi, n])
            st_scr[bi] = st * dl_scr[bi, n] + upd * bd_mask
        return carry

    lax.fori_loop(0, nch, chunk, 0)
    oo = o_scr[...].reshape(rows, width)
    ms = _dot_split(oo * oo, ones_bd) * (1.0 / HG_DV)
    on = oo * lax.rsqrt(ms + EPS) * gout_ref[...]
    o_ref[...] = (on * _silu(hgate)).reshape(bt, tl, width)

    @pl.when(j == pl.num_programs(1) - 1)
    def _():
        st_ref[...] = st_scr[...]


def _hgrn(hgc, hg_lb, gout_row, ones_bd, st0, bt, tl, c, l_valid, layer):
    b, l, _ = hgc.shape
    width = HG_H * HG_DK
    nch = tl // c
    return pl.pallas_call(
        functools.partial(_hgrn_kernel, bt=bt, tl=tl, c=c, l_valid=l_valid, layer=layer),
        out_shape=[jax.ShapeDtypeStruct((b, l, width), F32), jax.ShapeDtypeStruct((b, width, width), F32)],
        grid=(b // bt, l // tl),
        in_specs=[pl.BlockSpec((bt, tl, HG_COLS), lambda i, j: (i, j, 0)),
                  pl.BlockSpec(hg_lb.shape, lambda i, j: (0, 0)),
                  pl.BlockSpec((1, width), lambda i, j: (0, 0)),
                  pl.BlockSpec((width, width), lambda i, j: (0, 0)),
                  pl.BlockSpec((bt, width, width), lambda i, j: (i, 0, 0))],
        out_specs=[pl.BlockSpec((bt, tl, width), lambda i, j: (i, j, 0)),
                   pl.BlockSpec((bt, width, width), lambda i, j: (i, 0, 0))],
        scratch_shapes=[pltpu.VMEM((bt, width, width), F32),
                        pltpu.VMEM((bt, nch, c, width), BF16),
                        pltpu.VMEM((bt, nch, c, width), BF16),
                        pltpu.VMEM((bt, nch, c, width), BF16),
                        pltpu.VMEM((bt, nch, 1, width), F32),
                        pltpu.VMEM((bt, nch, c, width), F32)],
        compiler_params=_cparams("parallel", "arbitrary"),
        name="hgrn2_chunked",
    )(hgc, hg_lb, gout_row, ones_bd, st0)


def _s5_kernel(u_ref, perm_ref, permt_ref, br_ref, bi_ref, are_ref, aim_ref, ldt_ref, cre_ref, cim_ref, d_ref,
               wglu_ref, bglu_ref, x0r_ref, x0i_ref, y_ref, xr_out, xi_out,
               p_scr, q_scr, xr_scr, xi_scr, cr_scr, ci_scr, *, bt, tl):
    j = pl.program_id(1)
    n = bt * tl

    @pl.when(j == 0)
    def _():
        cr_scr[...] = x0r_ref[...]
        ci_scr[...] = x0i_ref[...]

    u = u_ref[...].reshape(n, S5_COLS)
    u_tm = _dot(perm_ref[...], u.astype(BF16)).astype(BF16)
    p_scr[...] = _dot(u_tm, br_ref[...])
    q_scr[...] = _dot(u_tm, bi_ref[...])
    ar, ai = are_ref[...], aim_ref[...]
    dt = jnp.exp(ldt_ref[...])
    mag = jnp.exp(dt * ar)
    abr1, abi1 = mag * jnp.cos(dt * ai), mag * jnp.sin(dt * ai)
    den = ar * ar + ai * ai
    fr1 = ((abr1 - 1.0) * ar + abi1 * ai) / den
    fi1 = (abi1 * ar - (abr1 - 1.0) * ai) / den
    shape = (bt, S5_STATE)
    abr, abi = jnp.broadcast_to(abr1, shape), jnp.broadcast_to(abi1, shape)
    fr, fi = jnp.broadcast_to(fr1, shape), jnp.broadcast_to(fi1, shape)

    def step(t, carry):
        xr, xi = carry
        off = pl.multiple_of(t * bt, bt)
        p = p_scr[pl.ds(off, bt), :]
        q = q_scr[pl.ds(off, bt), :]
        nxr = abr * xr - abi * xi + (fr * p - fi * q)
        nxi = abr * xi + abi * xr + (fr * q + fi * p)
        xr_scr[pl.ds(off, bt), :] = nxr
        xi_scr[pl.ds(off, bt), :] = nxi
        return nxr, nxi

    xr, xi = lax.fori_loop(0, tl, step, (cr_scr[...], ci_scr[...]))
    cr_scr[...] = xr
    ci_scr[...] = xi
    yc_tm = _dot(xr_scr[...].astype(BF16), cre_ref[...]) - _dot(xi_scr[...].astype(BF16), cim_ref[...])
    hi = yc_tm.astype(BF16)
    lo = (yc_tm - hi.astype(F32)).astype(BF16)
    yc = _dot(permt_ref[...], hi) + _dot(permt_ref[...], lo)
    y = jax.nn.gelu(yc + d_ref[...] * u)
    y = y * jax.nn.sigmoid(_dot(y.astype(BF16), wglu_ref[...]) + bglu_ref[...])
    y_ref[...] = y.reshape(bt, tl, S5_COLS)

    @pl.when(j == pl.num_programs(1) - 1)
    def _():
        xr_out[...] = xr
        xi_out[...] = xi


def _s5(u, perm, permt, br_bd, bi_bd, are_row, aim_row, ldt_row, cre_bd, cim_bd, d_row, wglu, bglu_row, x0r, x0i, bt, tl):
    b, l, _ = u.shape
    n = bt * tl
    const = lambda a, bb: pl.BlockSpec((a, bb), lambda i, j: (0, 0))
    state = pl.BlockSpec((bt, S5_STATE), lambda i, j: (i, 0))
    return pl.pallas_call(
        functools.partial(_s5_kernel, bt=bt, tl=tl),
        out_shape=[jax.ShapeDtypeStruct((b, l, S5_COLS), F32),
                   jax.ShapeDtypeStruct((b, S5_STATE), F32), jax.ShapeDtypeStruct((b, S5_STATE), F32)],
        grid=(b // bt, l // tl),
        in_specs=[pl.BlockSpec((bt, tl, S5_COLS), lambda i, j: (i, j, 0)),
                  const(n, n), const(n, n), const(S5_COLS, S5_STATE), const(S5_COLS, S5_STATE),
                  const(1, S5_STATE), const(1, S5_STATE), const(1, S5_STATE),
                  const(S5_STATE, S5_COLS), const(S5_STATE, S5_COLS), const(1, S5_COLS),
                  const(S5_COLS, S5_COLS), const(1, S5_COLS), state, state],
        out_specs=[pl.BlockSpec((bt, tl, S5_COLS), lambda i, j: (i, j, 0)), state, state],
        scratch_shapes=[pltpu.VMEM((n, S5_STATE), F32), pltpu.VMEM((n, S5_STATE), F32),
                        pltpu.VMEM((n, S5_STATE), F32), pltpu.VMEM((n, S5_STATE), F32),
                        pltpu.VMEM((bt, S5_STATE), F32), pltpu.VMEM((bt, S5_STATE), F32)],
        compiler_params=_cparams("parallel", "arbitrary"),
        name="s5_scan",
    )(u, perm, permt, br_bd, bi_bd, are_row, aim_row, ldt_row, cre_bd, cim_bd, d_row, wglu, bglu_row, x0r, x0i)


def _alibi_rows(rows, t):
    head = lax.broadcasted_iota(jnp.int32, (rows, 1), 0) // t
    slope = jnp.zeros((rows, 1), F32)
    for h in range(NSA_H):
        slope = jnp.where(head == h, 2.0 ** (-8.0 * (h + 1) / NSA_H), slope)
    return slope


def _stack_heads(x, width):
    return jnp.concatenate([x[:, h * width:(h + 1) * width] for h in range(NSA_H)], axis=0)


def _nsa_compressed(q, kcm, vcm, qpos, slope, t, pair, first_cmp_end):
    rows, ncmp = q.shape[0], kcm.shape[0]
    cend = lax.broadcasted_iota(jnp.int32, (rows, ncmp), 1) * NSA_CMP + first_cmp_end
    dc = qpos - cend
    mask = dc >= 0
    s = _dot_nt(q, kcm) - slope * dc.astype(F32)
    s = jnp.where(mask, s, NEG)
    m = jnp.max(s, axis=-1, keepdims=True)
    e = jnp.where(mask, jnp.exp(s - m), 0.0)
    p = e / jnp.maximum(jnp.sum(e, axis=-1, keepdims=True), TINY)
    o_c = _dot(p.astype(BF16), vcm)
    psum = p[0:t]
    for h in range(1, NSA_H):
        psum = psum + p[h * t:(h + 1) * t]
    return o_c, _dot_split(psum, pair)


def _nsa_select(imp, qpos_t, n_top):
    t, ns = imp.shape
    blk = lax.broadcasted_iota(jnp.int32, (t, ns), 1)
    imp = jnp.where(blk == qpos_t // NSA_SEL, jnp.inf, jnp.where(blk * NSA_SEL <= qpos_t, imp, -jnp.inf))
    rank = jnp.zeros((t, ns), jnp.int32)
    for c in range(ns):
        col = imp[:, c:c + 1]
        beats = jnp.where(col > imp, 1, jnp.where(col == imp, jnp.where(blk > c, 1, 0), 0))
        rank = rank + beats
    return jnp.where(rank < n_top, 1.0, 0.0)


def _nsa_combine(gates, o_c, o_s, o_w, t):
    outs = []
    for h in range(NSA_H):
        lane = PC_GATE_LANE + 3 * h
        rs = slice(h * t, (h + 1) * t)
        outs.append(gates[:, lane:lane + 1] * o_c[rs] + gates[:, lane + 1:lane + 2] * o_s[rs]
                    + gates[:, lane + 2:lane + 3] * o_w[rs])
    return jnp.concatenate(outs, axis=-1)


def _nsa_prompt_kernel(q_ref, rows_ref, win_ref, gate_ref, pair_ref, expand_ref, o_ref, cm_scr, *, tq, tk, l):
    i = pl.program_id(1)
    ncmp = l // NSA_CMP
    nsel = l // NSA_SEL
    d = NSA_D

    @pl.when(i == 0)
    def _():
        kv = rows_ref[:, 0:2 * d]
        cm_scr[...] = jnp.sum(kv.reshape(ncmp, NSA_CMP, 2 * d), axis=1) * (1.0 / NSA_CMP)

    rows = NSA_H * tq
    q = _stack_heads(q_ref[...], d).astype(BF16)
    slope = _alibi_rows(rows, tq)
    qpos = i * tq + lax.rem(lax.broadcasted_iota(jnp.int32, (rows, 1), 0), tq)
    cm = cm_scr[...]
    o_c, imp = _nsa_compressed(q, cm[:, 0:d].astype(BF16), cm[:, d:2 * d].astype(BF16), qpos, slope, tq,
                               pair_ref[...], NSA_CMP - 1)
    sel = _nsa_select(imp, qpos[0:tq], min(NSA_TOPN, nsel)).astype(BF16)
    sel4 = jnp.concatenate([sel] * NSA_H, axis=0)
    init = (jnp.full((rows, 1), NEG, F32), jnp.zeros((rows, 1), F32), jnp.zeros((rows, d), F32))
    kofs = lax.broadcasted_iota(jnp.int32, (rows, tk), 1)

    def sel_body(c, carry):
        m, lsum, acc = carry
        start = pl.multiple_of(c * tk, tk)
        kblk = rows_ref[pl.ds(start, tk), 2 * d:3 * d].astype(BF16)
        vblk = rows_ref[pl.ds(start, tk), 3 * d:4 * d].astype(BF16)
        ds = qpos - (kofs + c * tk)
        s = _dot_nt(q, kblk) - slope * ds.astype(F32)
        selk = _dot(sel4, expand_ref[c])
        mask = (selk > 0.5) & (ds >= 0)
        return _softmax_step(s, mask, m, lsum, acc, vblk)

    n_chunks = (i * tq + tq + tk - 1) // tk
    m, lsum, acc = lax.fori_loop(0, n_chunks, sel_body, init)
    o_s = acc / jnp.maximum(lsum, TINY)

    m, lsum, acc = init
    wofs = lax.broadcasted_iota(jnp.int32, (rows, tq), 1)
    n_win = NSA_WINDOW // tq + 1
    for jj in range(n_win):
        w = i - (n_win - 1) + jj
        start = pl.multiple_of(jnp.maximum(w, 0) * tq, tq)
        kblk = win_ref[pl.ds(start, tq), 0:d].astype(BF16)
        vblk = win_ref[pl.ds(start, tq), d:2 * d].astype(BF16)
        dw = qpos - (wofs + start)
        s = _dot_nt(q, kblk) - slope * dw.astype(F32)
        dwv = dw + jnp.where(w >= 0, 0, 2 * NSA_WINDOW + tq)
        mask = (dw >= 0) & (dwv < NSA_WINDOW)
        m, lsum, acc = _softmax_step(s, mask, m, lsum, acc, vblk)
    o_w = acc / jnp.maximum(lsum, TINY)
    o_ref[...] = _nsa_combine(gate_ref[...], o_c, o_s, o_w, tq)


def _nsa_prompt(qn, rows, win, gates, pair, expand, tq, tk):
    b, l, _ = qn.shape
    d = NSA_D
    seq = lambda n: pl.BlockSpec((None, l, n), lambda bi, i: (bi, 0, 0))
    blk = lambda n: pl.BlockSpec((None, tq, n), lambda bi, i: (bi, i, 0))
    return pl.pallas_call(
        functools.partial(_nsa_prompt_kernel, tq=tq, tk=tk, l=l),
        out_shape=jax.ShapeDtypeStruct((b, l, NSA_H * d), F32),
        grid=(b, l // tq),
        in_specs=[blk(NSA_H * d), seq(4 * d), seq(2 * d), blk(V7X_LANES),
                  pl.BlockSpec(pair.shape, lambda bi, i: (0, 0)),
                  pl.BlockSpec(expand.shape, lambda bi, i: (0, 0, 0))],
        out_specs=blk(NSA_H * d),
        scratch_shapes=[pltpu.VMEM((l // NSA_CMP, 2 * d), F32)],
        compiler_params=_cparams("parallel", "arbitrary"),
        name="nsa_prompt_attention",
    )(qn, rows, win, gates, pair, expand)


TPAD = V7X_SUBLANES


def _page_specs(layer, npg, page, width, colblk):
    def spec(jj):
        return pl.BlockSpec((None, None, page, width),
                            lambda b, k, pt: (layer, pt[b, k * npg + jj], 0, colblk))
    return [spec(jj) for jj in range(npg)]


def _mla_decode_kernel(pt_ref, qlat_ref, qrope_ref, knew_ref, *rest, npg, page, l_new):
    page_refs = rest[:npg]
    o_ref, m_scr, l_scr, acc_scr = rest[npg:]
    k = pl.program_id(1)
    rows = MLA_H * TPAD

    @pl.when(k == 0)
    def _():
        m_scr[...] = jnp.full((rows, 1), NEG, F32)
        l_scr[...] = jnp.zeros((rows, 1), F32)
        acc_scr[...] = jnp.zeros((rows, MLA_KVLORA), F32)

    q = _stack_mla_q(qlat_ref[...], qrope_ref[...])
    pages = [r[...].astype(BF16) for r in page_refs]
    s = jnp.concatenate([_dot_nt(q, pg) for pg in pages], axis=-1)
    m = m_scr[...]
    m_new = jnp.maximum(m, jnp.max(s, axis=-1, keepdims=True))
    alpha = jnp.exp(m - m_new)
    p = jnp.exp(s - m_new)
    l_scr[...] = alpha * l_scr[...] + jnp.sum(p, axis=-1, keepdims=True)
    acc = alpha * acc_scr[...]
    for jj, pg in enumerate(pages):
        acc = acc + _dot(p[:, jj * page:(jj + 1) * page].astype(BF16), pg[:, :MLA_KVLORA])
    acc_scr[...] = acc
    m_scr[...] = m_new

    @pl.when(k == pl.num_programs(1) - 1)
    def _():
        knew = knew_ref[...].astype(BF16)
        sn = _dot_nt(q, knew)
        tq = lax.rem(lax.broadcasted_iota(jnp.int32, (rows, TPAD), 0), TPAD)
        tk = lax.broadcasted_iota(jnp.int32, (rows, TPAD), 1)
        mask = (tk <= tq) & (tk < l_new)
        mm, ll, aa = _softmax_step(sn, mask, m_scr[...], l_scr[...], acc_scr[...], knew[:, :MLA_KVLORA])
        o = aa / jnp.maximum(ll, TINY)
        o_ref[...] = jnp.concatenate([o[h * TPAD:(h + 1) * TPAD] for h in range(MLA_H)], axis=-1)


def _mla_decode(page_table, qlat, qrope, knew, cache, layer, npg, l_new):
    nb, n_pages = page_table.shape
    page = cache.shape[2]
    n_lat = MLA_H * MLA_KVLORA
    rows = MLA_H * TPAD
    grid_spec = pltpu.PrefetchScalarGridSpec(
        num_scalar_prefetch=1,
        grid=(nb, n_pages // npg),
        in_specs=[pl.BlockSpec((None, TPAD, n_lat), lambda b, k, pt: (b, 0, 0)),
                  pl.BlockSpec((None, TPAD, MLA_H * MLA_ROPE), lambda b, k, pt: (b, 0, 0)),
                  pl.BlockSpec((None, TPAD, MLA_ROW), lambda b, k, pt: (b, 0, 0))]
                 + _page_specs(layer, npg, page, MLA_ROW, 0),
        out_specs=pl.BlockSpec((None, TPAD, n_lat), lambda b, k, pt: (b, 0, 0)),
        scratch_shapes=[pltpu.VMEM((rows, 1), F32), pltpu.VMEM((rows, 1), F32),
                        pltpu.VMEM((rows, MLA_KVLORA), F32)])
    return pl.pallas_call(
        functools.partial(_mla_decode_kernel, npg=npg, page=page, l_new=l_new),
        out_shape=jax.ShapeDtypeStruct((nb, TPAD, n_lat), F32),
        grid_spec=grid_spec,
        compiler_params=_cparams("parallel", "arbitrary"),
        name="mla_paged_decode",
    )(page_table, qlat, qrope, knew, *([cache] * npg))


def _nsa_cmp_decode_kernel(pt_ref, q_ref, pool_ref, pair_ref, *rest, npg, page, past_len):
    page_refs = rest[:npg]
    oc_ref, sel_ref, cm_scr = rest[npg:]
    k = pl.program_id(1)
    d = NSA_D
    per_step = npg * page // NSA_CMP
    acc = jnp.zeros((per_step, 2 * d), F32)
    for jj, r in enumerate(page_refs):
        x = r[...]
        hi = x.astype(BF16)
        lo = (x - hi.astype(F32)).astype(BF16)
        acc = acc + _dot(pool_ref[jj], hi) + _dot(pool_ref[jj], lo)
    cm_scr[pl.ds(pl.multiple_of(k * per_step, per_step), per_step), :] = acc

    @pl.when(k == pl.num_programs(1) - 1)
    def _():
        rows = NSA_H * TPAD
        q = _stack_heads(q_ref[...], d).astype(BF16)
        slope = _alibi_rows(rows, TPAD)
        qpos = past_len + lax.rem(lax.broadcasted_iota(jnp.int32, (rows, 1), 0), TPAD)
        cm = cm_scr[...]
        o_c, imp = _nsa_compressed(q, cm[:, 0:d].astype(BF16), cm[:, d:2 * d].astype(BF16), qpos, slope, TPAD,
                                   pair_ref[...], NSA_CMP - 1)
        oc_ref[...] = o_c
        sel_ref[...] = _nsa_select(imp, qpos[0:TPAD], NSA_TOPN - 1)


def _nsa_cmp_decode(page_table, qn, pool, pair, cache4, layer, npg, past_len):
    nb, n_pages = page_table.shape
    page = cache4.shape[2]
    d = NSA_D
    ncmp = past_len // NSA_CMP
    nsel = past_len // NSA_SEL
    rows = NSA_H * TPAD
    grid_spec = pltpu.PrefetchScalarGridSpec(
        num_scalar_prefetch=1,
        grid=(nb, n_pages // npg),
        in_specs=[pl.BlockSpec((None, TPAD, NSA_H * d), lambda b, k, pt: (b, 0, 0)),
                  pl.BlockSpec(pool.shape, lambda b, k, pt: (0, 0, 0)),
                  pl.BlockSpec(pair.shape, lambda b, k, pt: (0, 0))]
                 + _page_specs(layer, npg, page, 2 * d, 0),
        out_specs=[pl.BlockSpec((None, rows, d), lambda b, k, pt: (b, 0, 0)),
                   pl.BlockSpec((None, TPAD, nsel), lambda b, k, pt: (b, 0, 0))],
        scratch_shapes=[pltpu.VMEM((ncmp, 2 * d), F32)])
    return pl.pallas_call(
        functools.partial(_nsa_cmp_decode_kernel, npg=npg, page=page, past_len=past_len),
        out_shape=[jax.ShapeDtypeStruct((nb, rows, d), F32), jax.ShapeDtypeStruct((nb, TPAD, nsel), F32)],
        grid_spec=grid_spec,
        compiler_params=_cparams("parallel", "arbitrary"),
        name="nsa_compressed_decode",
    )(page_table, qn, pool, pair, *([cache4] * npg))


def _nsa_sel_decode_kernel(pt_ref, q_ref, sel_ref, expand_ref, oc_ref, rnew_ref, wpast_ref, wnew_ref, gate_ref, *rest,
                           npg, page, past_len, l_new):
    page_refs = rest[:npg]
    o_ref, m_scr, l_scr, acc_scr = rest[npg:]
    k = pl.program_id(1)
    d = NSA_D
    rows = NSA_H * TPAD

    @pl.when(k == 0)
    def _():
        m_scr[...] = jnp.full((rows, 1), NEG, F32)
        l_scr[...] = jnp.zeros((rows, 1), F32)
        acc_scr[...] = jnp.zeros((rows, d), F32)

    q = _stack_heads(q_ref[...], d).astype(BF16)
    slope = _alibi_rows(rows, TPAD)
    tq = lax.rem(lax.broadcasted_iota(jnp.int32, (rows, 1), 0), TPAD)
    qpos = past_len + tq
    pages = [r[...].astype(BF16) for r in page_refs]
    width = npg * page
    s = jnp.concatenate([_dot_nt(q, pg[:, 0:d]) for pg in pages], axis=-1)
    kpos = k * width + lax.broadcasted_iota(jnp.int32, (rows, width), 1)
    s = s - slope * (qpos - kpos).astype(F32)
    sel = sel_ref[...].astype(BF16)
    selk = _dot(jnp.concatenate([sel] * NSA_H, axis=0), expand_ref[...])
    mask = selk > 0.5
    s = jnp.where(mask, s, NEG)
    m = m_scr[...]
    m_new = jnp.maximum(m, jnp.max(s, axis=-1, keepdims=True))
    alpha = jnp.exp(m - m_new)
    p = jnp.where(mask, jnp.exp(s - m_new), 0.0)
    l_scr[...] = alpha * l_scr[...] + jnp.sum(p, axis=-1, keepdims=True)
    acc = alpha * acc_scr[...]
    for jj, pg in enumerate(pages):
        acc = acc + _dot(p[:, jj * page:(jj + 1) * page].astype(BF16), pg[:, d:2 * d])
    acc_scr[...] = acc
    m_scr[...] = m_new

    @pl.when(k == pl.num_programs(1) - 1)
    def _():
        tk = lax.broadcasted_iota(jnp.int32, (rows, TPAD), 1)
        new_mask = (tk <= tq) & (tk < l_new)
        new_bias = slope * (tq - tk).astype(F32)
        rnew = rnew_ref[...].astype(BF16)
        sn = _dot_nt(q, rnew[:, 2 * d:3 * d]) - new_bias
        mm, ll, aa = _softmax_step(sn, new_mask, m_scr[...], l_scr[...], acc_scr[...], rnew[:, 3 * d:4 * d])
        o_s = aa / jnp.maximum(ll, TINY)
        wpast = wpast_ref[...].astype(BF16)
        wlen = wpast.shape[0]
        wofs = lax.broadcasted_iota(jnp.int32, (rows, wlen), 1)
        dw = tq + (wlen - wofs)
        sw = _dot_nt(q, wpast[:, 0:d]) - slope * dw.astype(F32)
        init = (jnp.full((rows, 1), NEG, F32), jnp.zeros((rows, 1), F32), jnp.zeros((rows, d), F32))
        mm, ll, aa = _softmax_step(sw, dw < NSA_WINDOW, *init, wpast[:, d:2 * d])
        wnew = wnew_ref[...].astype(BF16)
        swn = _dot_nt(q, wnew[:, 0:d]) - new_bias
        mm, ll, aa = _softmax_step(swn, new_mask, mm, ll, aa, wnew[:, d:2 * d])
        o_w = aa / jnp.maximum(ll, TINY)
        o_ref[...] = _nsa_combine(gate_ref[...], oc_ref[...], o_s, o_w, TPAD)


def _nsa_sel_decode(page_table, qn, sel_steps, expand, o_c, rows_new, win_past, win_new, gates, cache4, layer, npg,
                    past_len, l_new):
    nb, n_pages = page_table.shape
    page = cache4.shape[2]
    d = NSA_D
    rows = NSA_H * TPAD
    wlen = win_past.shape[1]
    per_b = lambda a, n: pl.BlockSpec((None, a, n), lambda b, k, pt: (b, 0, 0))
    grid_spec = pltpu.PrefetchScalarGridSpec(
        num_scalar_prefetch=1,
        grid=(nb, n_pages // npg),
        in_specs=[per_b(TPAD, NSA_H * d),
                  pl.BlockSpec((None, None, TPAD, sel_steps.shape[3]), lambda b, k, pt: (b, k, 0, 0)),
                  pl.BlockSpec(expand.shape, lambda b, k, pt: (0, 0)),
                  per_b(rows, d), per_b(TPAD, 4 * d), per_b(wlen, 2 * d), per_b(TPAD, 2 * d), per_b(TPAD, V7X_LANES)]
                 + _page_specs(layer, npg, page, 2 * d, 1),
        out_specs=per_b(TPAD, NSA_H * d),
        scratch_shapes=[pltpu.VMEM((rows, 1), F32), pltpu.VMEM((rows, 1), F32), pltpu.VMEM((rows, d), F32)])
    return pl.pallas_call(
        functools.partial(_nsa_sel_decode_kernel, npg=npg, page=page, past_len=past_len, l_new=l_new),
        out_shape=jax.ShapeDtypeStruct((nb, TPAD, NSA_H * d), F32),
        grid_spec=grid_spec,
        compiler_params=_cparams("parallel", "arbitrary"),
        name="nsa_selected_window_decode",
    )(page_table, qn, sel_steps, expand, o_c, rows_new, win_past, win_new, gates, *([cache4] * npg))


def _out_proj_kernel(x_ref, olat_ref, ohg_ref, onsa_ref, os5_ref, wuv_ref, wout_ref, g1_ref, sc2_ref, sh2_ref,
                     gffn_ref, x1_ref, h2_ref):
    w = NSA_H * NSA_D
    omla = _dot(olat_ref[...].astype(BF16), wuv_ref[...])
    mix = (_dot(omla.astype(BF16), wout_ref[0:w, :]) + _dot(ohg_ref[...].astype(BF16), wout_ref[w:2 * w, :])
           + _dot(onsa_ref[...].astype(BF16), wout_ref[2 * w:3 * w, :])
           + _dot(os5_ref[...].astype(BF16), wout_ref[3 * w:4 * w, :]))
    x1 = x_ref[...] + g1_ref[...] * mix
    x1_ref[...] = x1
    var = jnp.mean(x1 * x1, axis=-1, keepdims=True)
    h2 = (x1 * lax.rsqrt(var + EPS) * gffn_ref[...]) * (1.0 + sc2_ref[...]) + sh2_ref[...]
    h2_ref[...] = h2.astype(BF16)


def _out_proj(x2d, olat, ohg, onsa, os5, wuv_bd, wout, g1, sc2, sh2, g_ffn, rows_per_seq, tm):
    r, d = x2d.shape
    if rows_per_seq is not None:
        tps = rows_per_seq // tm
        mod_spec = pl.BlockSpec((None, 1, d), lambda i: (i // tps, 0, 0))
    else:
        mod_spec = pl.BlockSpec((tm, d), lambda i: (i, 0))
    row = lambda n: pl.BlockSpec((tm, n), lambda i: (i, 0))
    const = lambda a, b: pl.BlockSpec((a, b), lambda i: (0, 0))
    w = NSA_H * NSA_D
    return pl.pallas_call(
        _out_proj_kernel,
        out_shape=[jax.ShapeDtypeStruct((r, d), F32), jax.ShapeDtypeStruct((r, d), BF16)],
        grid=(r // tm,),
        in_specs=[row(d), row(MLA_H * MLA_KVLORA), row(w), row(w), row(w), const(MLA_H * MLA_KVLORA, w),
                  const(4 * w, d), mod_spec, mod_spec, mod_spec, const(1, d)],
        out_specs=[row(d), row(d)],
        compiler_params=_cparams("parallel"),
        name="out_proj_residual",
    )(x2d, olat, ohg, onsa, os5, wuv_bd, wout, g1, sc2, sh2, g_ffn)


def _ffn_tail(j, upg_scr, upv_scr, offs, tm, cwg_ref, cwv_ref, cbg_ref, cbv_ref, wd_ref, x1_ref, g2_ref, o_ref,
              acc_scr):
    g = cbg_ref[...]
    v = cbv_ref[...]
    for k in range(CONV_W):
        g = g + upg_scr[pl.ds(offs[k], tm), :] * cwg_ref[k:k + 1, :]
        v = v + upv_scr[pl.ds(offs[k], tm), :] * cwv_ref[k:k + 1, :]
    part = _dot((_silu(g) * v).astype(BF16), wd_ref[...])

    @pl.when(j == 0)
    def _():
        acc_scr[...] = part

    @pl.when(j > 0)
    def _():
        acc_scr[...] = acc_scr[...] + part

    @pl.when(j == pl.num_programs(1) - 1)
    def _():
        o_ref[...] = x1_ref[...] + g2_ref[...] * acc_scr[...]


def _ffn_seq_kernel(h2_ref, halo_ref, x1_ref, g2_ref, wg_ref, wv_ref, cwg_ref, cwv_ref, cbg_ref, cbv_ref, wd_ref,
                    o_ref, sg_ref, sv_ref, upg_scr, upv_scr, acc_scr, *, tm, tiles_per_seq, halo):
    i, j = pl.program_id(0), pl.program_id(1)
    he = jnp.concatenate([halo_ref[...], h2_ref[...]], axis=0)
    rowid = lax.broadcasted_iota(jnp.int32, (halo + tm, 1), 0)
    keep = jnp.logical_or(rowid >= halo, lax.rem(i, tiles_per_seq) != 0)
    upg_scr[...] = jnp.where(keep, _dot(he, wg_ref[...]), 0.0)
    upv_scr[...] = jnp.where(keep, _dot(he, wv_ref[...]), 0.0)
    sg_ref[...] = upg_scr[pl.ds(halo + tm - (CONV_W - 1), CONV_W - 1), :]
    sv_ref[...] = upv_scr[pl.ds(halo + tm - (CONV_W - 1), CONV_W - 1), :]
    offs = tuple(halo - (CONV_W - 1) + k for k in range(CONV_W))
    _ffn_tail(j, upg_scr, upv_scr, offs, tm, cwg_ref, cwv_ref, cbg_ref, cbv_ref, wd_ref, x1_ref, g2_ref, o_ref,
              acc_scr)


def _ffn_seq(h2, x1, g2, wg, wv, cwg, cwv, cbg, cbv, wd, rows_per_seq, tm, ck):
    r, d = x1.shape
    dff = wd.shape[0]
    nj = dff // ck
    tps = rows_per_seq // tm
    halo = 16
    hb = tm // halo
    xo, sg, sv = pl.pallas_call(
        functools.partial(_ffn_seq_kernel, tm=tm, tiles_per_seq=tps, halo=halo),
        out_shape=[jax.ShapeDtypeStruct((r, d), F32),
                   jax.ShapeDtypeStruct((r // tm, CONV_W - 1, dff), F32),
                   jax.ShapeDtypeStruct((r // tm, CONV_W - 1, dff), F32)],
        grid=(r // tm, nj),
        in_specs=[pl.BlockSpec((tm, d), lambda i, j: (i, 0)),
                  pl.BlockSpec((halo, d), lambda i, j: (jnp.maximum(i * hb - 1, 0), 0)),
                  pl.BlockSpec((tm, d), lambda i, j: (i, 0)),
                  pl.BlockSpec((None, 1, d), lambda i, j: (i // tps, 0, 0)),
                  pl.BlockSpec((d, ck), lambda i, j: (0, j)),
                  pl.BlockSpec((d, ck), lambda i, j: (0, j)),
                  pl.BlockSpec((CONV_W, ck), lambda i, j: (0, j)),
                  pl.BlockSpec((CONV_W, ck), lambda i, j: (0, j)),
                  pl.BlockSpec((1, ck), lambda i, j: (0, j)),
                  pl.BlockSpec((1, ck), lambda i, j: (0, j)),
                  pl.BlockSpec((ck, d), lambda i, j: (j, 0))],
        out_specs=[pl.BlockSpec((tm, d), lambda i, j: (i, 0)),
                   pl.BlockSpec((None, CONV_W - 1, ck), lambda i, j: (i, 0, j)),
                   pl.BlockSpec((None, CONV_W - 1, ck), lambda i, j: (i, 0, j))],
        scratch_shapes=[pltpu.VMEM((halo + tm, ck), F32), pltpu.VMEM((halo + tm, ck), F32),
                        pltpu.VMEM((tm, d), F32)],
        compiler_params=_cparams("parallel", "arbitrary"),
        name="ffn_conv_prompt",
    )(h2, h2, x1, g2, wg, wv, cwg, cwv, cbg, cbv, wd)
    return xo, sg[tps - 1::tps], sv[tps - 1::tps]


def _ffn_step_kernel(h2_ref, sg_in_ref, sv_in_ref, x1_ref, g2_ref, wg_ref, wv_ref, cwg_ref, cwv_ref, cbg_ref, cbv_ref,
                     wd_ref, o_ref, sg_ref, sv_ref, upg_scr, upv_scr, acc_scr, *, tm, nb):
    j = pl.program_id(1)
    hist = (CONV_W - 1) * nb
    upg_scr[0:hist, :] = sg_in_ref[...]
    upv_scr[0:hist, :] = sv_in_ref[...]
    upg_scr[hist:hist + tm, :] = _dot(h2_ref[...], wg_ref[...])
    upv_scr[hist:hist + tm, :] = _dot(h2_ref[...], wv_ref[...])
    sg_ref[...] = upg_scr[tm:tm + hist, :]
    sv_ref[...] = upv_scr[tm:tm + hist, :]
    offs = tuple(k * nb for k in range(CONV_W))
    _ffn_tail(j, upg_scr, upv_scr, offs, tm, cwg_ref, cwv_ref, cbg_ref, cbv_ref, wd_ref, x1_ref, g2_ref, o_ref,
              acc_scr)


def _ffn_step(h2, sg_in, sv_in, x1, g2, wg, wv, cwg, cwv, cbg, cbv, wd, nb, ck):
    tm, d = x1.shape
    dff = wd.shape[0]
    nj = dff // ck
    hist = (CONV_W - 1) * nb
    full = lambda a, b: pl.BlockSpec((a, b), lambda i, j: (0, 0))
    colblk = lambda a: pl.BlockSpec((a, ck), lambda i, j: (0, j))
    return pl.pallas_call(
        functools.partial(_ffn_step_kernel, tm=tm, nb=nb),
        out_shape=[jax.ShapeDtypeStruct((tm, d), F32), jax.ShapeDtypeStruct((hist, dff), F32),
                   jax.ShapeDtypeStruct((hist, dff), F32)],
        grid=(1, nj),
        in_specs=[full(tm, d), colblk(hist), colblk(hist), full(tm, d), full(tm, d), colblk(d), colblk(d),
                  colblk(CONV_W), colblk(CONV_W), colblk(1), colblk(1),
                  pl.BlockSpec((ck, d), lambda i, j: (j, 0))],
        out_specs=[full(tm, d), colblk(hist), colblk(hist)],
        scratch_shapes=[pltpu.VMEM((hist + tm, ck), F32), pltpu.VMEM((hist + tm, ck), F32),
                        pltpu.VMEM((tm, d), F32)],
        compiler_params=_cparams("parallel", "arbitrary"),
        name="ffn_conv_step",
    )(h2, sg_in, sv_in, x1, g2, wg, wv, cwg, cwv, cbg, cbv, wd)


def _final_norm_kernel(x_ref, g_ref, o_ref):
    x = x_ref[...]
    var = jnp.mean(x * x, axis=-1, keepdims=True)
    o_ref[...] = x * lax.rsqrt(var + EPS) * g_ref[...]


def _final_norm(x2d, g, tm):
    r, d = x2d.shape
    return pl.pallas_call(
        _final_norm_kernel,
        out_shape=jax.ShapeDtypeStruct((r, d), F32),
        grid=(r // tm,),
        in_specs=[pl.BlockSpec((tm, d), lambda i: (i, 0)), pl.BlockSpec((1, d), lambda i: (0, 0))],
        out_specs=pl.BlockSpec((tm, d), lambda i: (i, 0)),
        compiler_params=_cparams("parallel"),
        name="final_rmsnorm",
    )(x2d, g)


def _rope_tables(pos):
    half = MLA_ROPE // 2
    inv = ROPE_BASE ** (-jnp.arange(half, dtype=F32) / half)
    ang = pos.astype(F32)[:, None] * inv[None, :]
    c, s = jnp.cos(ang), jnp.sin(ang)
    return jnp.concatenate([c, c], axis=-1), jnp.concatenate([-s, s], axis=-1)


def _swap_halves(w):
    half = w.shape[-1] // 2
    return jnp.concatenate([w[..., half:], w[..., :half]], axis=-1)


def _block_diag(blocks):
    g, a, b = blocks.shape
    eye = jnp.eye(g, dtype=blocks.dtype)
    return (blocks[:, :, None, :] * eye[:, None, :, None]).reshape(g * a, g * b)


def _time_major_perm(bt, tl):
    n = bt * tl
    src = jnp.arange(n)
    dst = (src % tl) * bt + src // tl
    return (dst[:, None] == jnp.arange(n)[None, :]).astype(BF16).T


def _layer_consts(l, w_in, w_out, mla_w_uq, mla_w_uk, mla_w_uv, s5_b_re, s5_b_im, s5_c_re, s5_c_im, ffn_w_up,
                  ffn_w_down):
    d = w_in.shape[1]
    w = w_in[l]
    o = 0
    cq = w[:, o:o + MLA_QLORA]; o += MLA_QLORA
    ckv = w[:, o:o + MLA_KVLORA]; o += MLA_KVLORA
    kr = w[:, o:o + MLA_ROPE]; o += MLA_ROPE
    hgw = w[:, o:o + HG_COLS]; o += HG_COLS
    qn = w[:, o:o + NSA_H * NSA_D]; o += NSA_H * NSA_D
    kvs = w[:, o:o + 6 * NSA_D]; o += 6 * NSA_D
    gates = w[:, o:o + 3 * NSA_H]; o += 3 * NSA_H
    s5w = w[:, o:o + S5_COLS]
    pad = jnp.zeros((d, V7X_LANES - 2 * MLA_ROPE - 3 * NSA_H), w.dtype)
    w_perm = jnp.concatenate([cq, ckv, kr, _swap_halves(kr), gates, pad, hgw, qn, kvs, s5w], axis=1).astype(BF16)
    uq = mla_w_uq[l].reshape(MLA_QLORA, MLA_H, MLA_NOPE + MLA_ROPE)
    uq_rope = uq[:, :, MLA_NOPE:]
    wuq_perm = jnp.concatenate([uq[:, :, :MLA_NOPE].reshape(MLA_QLORA, -1), uq_rope.reshape(MLA_QLORA, -1),
                                _swap_halves(uq_rope).reshape(MLA_QLORA, -1)], axis=1).astype(BF16)
    return dict(
        w_perm=w_perm, wuq_perm=wuq_perm,
        wuk_bd=_block_diag(mla_w_uk[l].transpose(1, 2, 0)).astype(BF16),
        wuv_bd=_block_diag(mla_w_uv[l].transpose(1, 0, 2)).astype(BF16),
        wout=w_out[l].astype(BF16),
        br_bd=_block_diag(s5_b_re[l].transpose(0, 2, 1)).astype(BF16),
        bi_bd=_block_diag(s5_b_im[l].transpose(0, 2, 1)).astype(BF16),
        cre_bd=_block_diag(s5_c_re[l].transpose(0, 2, 1)).astype(BF16),
        cim_bd=_block_diag(s5_c_im[l].transpose(0, 2, 1)).astype(BF16),
        wg=ffn_w_up[l][:, :ffn_w_down.shape[1]].astype(BF16),
        wv=ffn_w_up[l][:, ffn_w_down.shape[1]:].astype(BF16),
        wd=ffn_w_down[l].astype(BF16),
    )


def _hgrn_state_to_bd(s):
    b = s.shape[0]
    blocks = s.transpose(0, 1, 3, 2)
    eye = jnp.eye(HG_H, dtype=s.dtype)
    return (blocks[:, :, :, None, :] * eye[None, :, None, :, None]).reshape(b, HG_H * HG_DV, HG_H * HG_DK)


def _hgrn_state_from_bd(st):
    b = st.shape[0]
    st5 = st.reshape(b, HG_H, HG_DV, HG_H, HG_DK)
    return jnp.stack([st5[:, h, :, h, :] for h in range(HG_H)], axis=1).transpose(0, 1, 3, 2)


def _pad_t(x):
    return jnp.pad(x, ((0, 0), (0, TPAD - x.shape[1]), (0, 0)))


def kernel(x_prompt, x_sample, c_prompt, c_sample, cache_mla, cache_nsa, cache_nsa_win, state_hgrn, state_s5_re, state_s5_im, state_ffn_conv, page_table, w_ada, b_ada, g_mix, g_ffn, w_in, w_out, mla_g_q, mla_g_kv, mla_w_uq, mla_w_uk, mla_w_uv, hg_lb, hg_g_out, s5_a_re, s5_a_im, s5_log_dt, s5_b_re, s5_b_im, s5_c_re, s5_c_im, s5_d, s5_w_glu, s5_b_glu, ffn_w_up, ffn_conv_w, ffn_conv_b, ffn_w_down, g_final):
    depth = w_in.shape[0]
    bp, lp, d = x_prompt.shape
    nb, ls, _ = x_sample.shape
    page = cache_mla.shape[2]
    n_pages = page_table.shape[1]
    past_len = n_pages * page
    dff = ffn_w_down.shape[1]
    rp, rs = bp * lp, nb * ls
    assert ls <= TPAD and past_len % (NSA_SEL * 2) == 0 and lp % NSA_SEL == 0
    assert cache_nsa_win.shape[2] == NSA_WINDOW and past_len >= NSA_WINDOW

    tm_in, tm_row, tq_mla, tk_mla, tq_nsa, tk_nsa = 256, 512, 256, 256, 128, 512
    bt, tl_rec, c_hg = 8, 128, 16
    ck = 384
    npg = 16

    ones_bd = _block_diag(jnp.ones((HG_H, HG_DK, HG_DV), BF16))
    cos_p, sin_p = _rope_tables(jnp.arange(lp))
    cos_s, sin_s = _rope_tables(past_len + jnp.arange(ls))
    cos_s, sin_s = jnp.tile(cos_s, (nb, 1)), jnp.tile(sin_s, (nb, 1))
    tile_h = lambda t: jnp.tile(t, (1, MLA_H))
    perm_p = _time_major_perm(bt, tl_rec)
    perm_s = _time_major_perm(bt, ls)
    ncmp_p, nsel_p = lp // NSA_CMP, lp // NSA_SEL
    pair_p = (jnp.arange(ncmp_p)[:, None] // (NSA_SEL // NSA_CMP) == jnp.arange(nsel_p)[None, :]).astype(BF16)
    kidx = jnp.arange(lp).reshape(lp // tk_nsa, 1, tk_nsa)
    expand_p = (kidx // NSA_SEL == jnp.arange(nsel_p)[None, :, None]).astype(BF16)
    ncmp_s, nsel_s = past_len // NSA_CMP, past_len // NSA_SEL
    pair_s = (jnp.arange(ncmp_s)[:, None] // (NSA_SEL // NSA_CMP) == jnp.arange(nsel_s)[None, :]).astype(BF16)
    step_keys = npg * page
    bps = step_keys // NSA_SEL
    expand_s = (jnp.arange(step_keys)[None, :] // NSA_SEL == jnp.arange(bps)[:, None]).astype(BF16)
    per_step = step_keys // NSA_CMP
    pool = ((jnp.arange(per_step)[None, :, None]
             == (jnp.arange(npg)[:, None, None] * (page // NSA_CMP) + jnp.arange(page)[None, None, :] // NSA_CMP))
            .astype(F32) * (1.0 / NSA_CMP)).astype(BF16)
    cache4 = cache_nsa.reshape(cache_nsa.shape[0], cache_nsa.shape[1], page, 4 * NSA_D)

    mod = _modulation(jnp.concatenate([c_prompt, c_sample], axis=0), w_ada, b_ada)
    xp = x_prompt.reshape(rp, d)
    xs = x_sample.reshape(rs, d)
    leaves_p, leaves_s = [], []
    row1 = lambda v: v.reshape(1, -1)
    for l in range(depth):
        cst = _layer_consts(l, w_in, w_out, mla_w_uq, mla_w_uk, mla_w_uv, s5_b_re, s5_b_im, s5_c_re, s5_c_im,
                            ffn_w_up, ffn_w_down)
        gout_row = row1(jnp.tile(hg_g_out[l], HG_H))
        are_row, aim_row = row1(s5_a_re[l]), row1(s5_a_im[l])
        ldt_row = row1(jnp.repeat(s5_log_dt[l], S5_N))
        d_row, bglu_row = row1(s5_d[l]), row1(s5_b_glu[l])
        wglu = s5_w_glu[l].astype(BF16)
        cwg, cwv = ffn_conv_w[l][:, :dff], ffn_conv_w[l][:, dff:]
        cbg, cbv = row1(ffn_conv_b[l][:dff]), row1(ffn_conv_b[l][dff:])
        s5_args = (cst["br_bd"], cst["bi_bd"], are_row, aim_row, ldt_row, cst["cre_bd"], cst["cim_bd"], d_row, wglu,
                   bglu_row)

        m6 = mod[l, :bp].reshape(bp, 6, 1, d)
        sh1, sc1, g1, sh2, sc2, g2 = (m6[:, i] for i in range(6))
        cq, mla_rows, hgc, qn, nsa_rows, win, gates, s5u = _proj_in(
            xp, sc1, sh1, row1(g_mix[l]), cst["w_perm"], row1(mla_g_kv[l]), cos_p, sin_p, lp, tm_in)
        qlat, qrope = _mla_q(cq, row1(mla_g_q[l]), cst["wuq_perm"], cst["wuk_bd"], tile_h(cos_p), tile_h(sin_p), lp,
                             tm_row)
        olat = _mla_prompt(qlat.reshape(bp, lp, -1), qrope.reshape(bp, lp, -1), mla_rows.reshape(bp, lp, -1),
                           tq_mla, tk_mla)
        ohg, st = _hgrn(hgc.reshape(bp, lp, -1), hg_lb, gout_row, ones_bd,
                        jnp.zeros((bp, HG_H * HG_DV, HG_H * HG_DK), F32), bt, tl_rec, c_hg, lp, l)
        onsa = _nsa_prompt(qn.reshape(bp, lp, -1), nsa_rows.reshape(bp, lp, -1), win.reshape(bp, lp, -1),
                           gates.reshape(bp, lp, -1), pair_p, expand_p, tq_nsa, tk_nsa)
        zero_state = jnp.zeros((bp, S5_STATE), F32)
        os5, s5r, s5i = _s5(s5u.reshape(bp, lp, -1), perm_p, perm_p.T, *s5_args, zero_state, zero_state, bt, tl_rec)
        x1, h2 = _out_proj(xp, olat.reshape(rp, -1), ohg.reshape(rp, -1), onsa.reshape(rp, -1), os5.reshape(rp, -1),
                           cst["wuv_bd"], cst["wout"], g1, sc2, sh2, row1(g_ffn[l]), lp, tm_row)
        xp, cg, cv = _ffn_seq(h2, x1, g2, cst["wg"], cst["wv"], cwg, cwv, cbg, cbv, cst["wd"], lp, tm_row, ck)
        wlen = min(NSA_WINDOW, lp)
        leaves_p.append(dict(
            mla=mla_rows.reshape(bp, lp, MLA_ROW), nsa=nsa_rows.reshape(bp, lp, 4, NSA_D),
            win=win.reshape(bp, lp, 2, NSA_D)[:, lp - wlen:], hgrn=_hgrn_state_from_bd(st),
            s5_re=s5r.reshape(bp, S5_G, S5_N), s5_im=s5i.reshape(bp, S5_G, S5_N),
            conv=jnp.concatenate([cg, cv], axis=-1)))

        m6 = jnp.repeat(mod[l, bp:].reshape(nb, 6, d), ls, axis=0)
        sh1, sc1, g1, sh2, sc2, g2 = (m6[:, i] for i in range(6))
        cq, mla_rows, hgc, qn, nsa_rows, win, gates, s5u = _proj_in(
            xs, sc1, sh1, row1(g_mix[l]), cst["w_perm"], row1(mla_g_kv[l]), cos_s, sin_s, None, rs)
        qlat, qrope = _mla_q(cq, row1(mla_g_q[l]), cst["wuq_perm"], cst["wuk_bd"], tile_h(cos_s), tile_h(sin_s),
                             None, rs)
        seq = lambda a: _pad_t(a.reshape(nb, ls, -1))
        olat = _mla_decode(page_table, seq(qlat), seq(qrope), seq(mla_rows), cache_mla, l, npg, ls)[:, :ls]
        ohg, st = _hgrn(seq(hgc), hg_lb, gout_row, ones_bd, _hgrn_state_to_bd(state_hgrn[l]), bt, TPAD, TPAD, ls, l)
        ohg = ohg[:, :ls]
        o_c, sel = _nsa_cmp_decode(page_table, seq(qn), pool, pair_s, cache4, l, npg, past_len)
        sel_steps = sel.reshape(nb, TPAD, n_pages // npg, bps).transpose(0, 2, 1, 3)
        win_past = cache_nsa_win[l].reshape(nb, NSA_WINDOW, 2 * NSA_D)
        onsa = _nsa_sel_decode(page_table, seq(qn), sel_steps, expand_s, o_c, seq(nsa_rows), win_past, seq(win),
                               seq(gates), cache4, l, npg, past_len, ls)[:, :ls]
        os5, s5r, s5i = _s5(s5u.reshape(nb, ls, -1), perm_s, perm_s.T, *s5_args,
                            state_s5_re[l].reshape(nb, S5_STATE), state_s5_im[l].reshape(nb, S5_STATE), bt, ls)
        x1, h2 = _out_proj(xs, olat.reshape(rs, -1), ohg.reshape(rs, -1), onsa.reshape(rs, -1), os5.reshape(rs, -1),
                           cst["wuv_bd"], cst["wout"], g1, sc2, sh2, row1(g_ffn[l]), None, rs)
        tmaj = lambda a: a.reshape(nb, ls, -1).transpose(1, 0, 2).reshape(rs, -1)
        conv_in = state_ffn_conv[l].transpose(1, 0, 2).reshape((CONV_W - 1) * nb, 2 * dff)
        xs_tm, sg, sv = _ffn_step(tmaj(h2), conv_in[:, :dff], conv_in[:, dff:], tmaj(x1), tmaj(g2), cst["wg"],
                                  cst["wv"], cwg, cwv, cbg, cbv, cst["wd"], nb, ck)
        xs = xs_tm.reshape(ls, nb, d).transpose(1, 0, 2).reshape(rs, d)
        conv_out = jnp.concatenate([sg, sv], axis=-1).reshape(CONV_W - 1, nb, 2 * dff).transpose(1, 0, 2)
        win_state = jnp.concatenate([cache_nsa_win[l], win.reshape(nb, ls, 2, NSA_D)], axis=1)[:, ls:]
        leaves_s.append(dict(
            mla=mla_rows.reshape(nb, ls, MLA_ROW), nsa=nsa_rows.reshape(nb, ls, 4, NSA_D), win=win_state,
            hgrn=_hgrn_state_from_bd(st), s5_re=s5r.reshape(nb, S5_G, S5_N), s5_im=s5i.reshape(nb, S5_G, S5_N),
            conv=conv_out))

    y_prompt = _final_norm(xp, row1(g_final), tm_row).reshape(bp, lp, d)
    y_sample = _final_norm(xs, row1(g_final), rs).reshape(nb, ls, d)
    stk = lambda lst, name: jnp.stack([dd[name] for dd in lst], axis=0)
    out = [y_prompt, y_sample]
    for name in ("mla", "nsa", "win", "hgrn", "s5_re", "s5_im", "conv"):
        out += [stk(leaves_p, name), stk(leaves_s, name)]
    return tuple(out)
```

```python
import functools

import jax
import jax.numpy as jnp
from jax import lax
from jax.experimental import pallas as pl
from jax.experimental.pallas import tpu as pltpu

F32 = jnp.float32
BF16 = jnp.bfloat16

MLA_H, MLA_NOPE, MLA_ROPE, MLA_V = 4, 64, 32, 64
MLA_QLORA, MLA_KVLORA = 384, 128
ROPE_BASE = 10000.0
HG_H, HG_DK, HG_DV = 4, 64, 64
NSA_H, NSA_D, NSA_CMP, NSA_SEL, NSA_TOPN, NSA_WINDOW = 4, 64, 32, 64, 16, 512
S5_CH, S5_G, S5_N = 16, 16, 64
CONV_W = 3
EPS = 1e-6
NEG = -1e30
TINY = 1e-30
BIAS_BIG = 2.0 ** 100

HG_COLS = 2 * HG_H * HG_DK + 2 * HG_H * HG_DV
S5_COLS = S5_G * S5_CH
MLA_ROW = MLA_KVLORA + MLA_ROPE
S5_STATE = S5_G * S5_N

V7X_LANES = 128
V7X_SUBLANES = 8
V7X_VMEM_LIMIT_BYTES = 56 * 1024 * 1024

PC_CQ = 0
PC_CKV = 384
PC_MISC = 512
PC_GATE_LANE = 64
PC_HG = 640
PC_QN = 1664
PC_NSA = 1920
PC_WIN = 2176
PC_S5 = 2304
PC_TOTAL = 2560

NSA_XW = 64
NSA_X_BLOCKS = 32
NSA_X_HI = 32
NSA_X_LO = 33
NSA_X_ONE = 34


def _cparams(*semantics):
    return pltpu.CompilerParams(dimension_semantics=tuple(semantics), vmem_limit_bytes=V7X_VMEM_LIMIT_BYTES)


def _dot(a, b):
    return jnp.dot(a, b, preferred_element_type=F32)


def _dot_nt(a, b):
    return lax.dot_general(a, b, (((1,), (1,)), ((), ())), preferred_element_type=F32)


def _dot_tn(a, b):
    return lax.dot_general(a, b, (((0,), (0,)), ((), ())), preferred_element_type=F32)


def _split(a):
    hi = a.astype(BF16)
    return hi, (a - hi.astype(F32)).astype(BF16)


def _dot_split(a, b_bf16):
    hi, lo = _split(a)
    return _dot(hi, b_bf16) + _dot(lo, b_bf16)


def _silu(x):
    return x * jax.nn.sigmoid(x)


def _mod_kernel(c_ref, w_ref, b_ref, o_ref):
    s = _silu(c_ref[...]).astype(BF16)
    o_ref[...] = _dot(s, w_ref[...].astype(BF16)) + b_ref[...]


def _modulation(c_all, w_ada, b_ada):
    depth, d, n = w_ada.shape
    bc = c_all.shape[0]
    tn = 1536
    return pl.pallas_call(
        _mod_kernel,
        out_shape=jax.ShapeDtypeStruct((depth, bc, n), F32),
        grid=(depth, n // tn),
        in_specs=[pl.BlockSpec((bc, d), lambda l, j: (0, 0)),
                  pl.BlockSpec((None, d, tn), lambda l, j: (l, 0, j)),
                  pl.BlockSpec((None, 1, tn), lambda l, j: (l, 0, j))],
        out_specs=pl.BlockSpec((None, bc, tn), lambda l, j: (l, 0, j)),
        compiler_params=_cparams("parallel", "parallel"),
        name="adaln_modulation",
    )(c_all, w_ada, b_ada.reshape(depth, 1, n))


def _proj_in_kernel(x_ref, sc_ref, sh_ref, g_ref, w_ref, gkv_ref, cos_ref, sin_ref,
                    cq_ref, mla_ref, hg_ref, qn_ref, nsa_ref, win_ref, gate_ref, s5_ref):
    x = x_ref[...]
    var = jnp.mean(x * x, axis=-1, keepdims=True)
    h = (x * lax.rsqrt(var + EPS) * g_ref[...]) * (1.0 + sc_ref[...]) + sh_ref[...]
    cols = _dot(h.astype(BF16), w_ref[...])
    cq_ref[...] = cols[:, PC_CQ:PC_CQ + MLA_QLORA]
    ckv = cols[:, PC_CKV:PC_CKV + MLA_KVLORA]
    kvar = jnp.mean(ckv * ckv, axis=-1, keepdims=True)
    mla_ref[:, 0:MLA_KVLORA] = ckv * lax.rsqrt(kvar + EPS) * gkv_ref[...]
    kr = cols[:, PC_MISC:PC_MISC + MLA_ROPE]
    kr_swapped = cols[:, PC_MISC + MLA_ROPE:PC_MISC + 2 * MLA_ROPE]
    mla_ref[:, MLA_KVLORA:MLA_ROW] = kr * cos_ref[...] + kr_swapped * sin_ref[...]
    gate_ref[...] = jax.nn.sigmoid(cols[:, PC_MISC:PC_MISC + V7X_LANES])
    hg_ref[...] = cols[:, PC_HG:PC_HG + HG_COLS]
    qn_ref[...] = cols[:, PC_QN:PC_QN + NSA_H * NSA_D] * (NSA_D ** -0.5)
    nsa_ref[...] = cols[:, PC_NSA:PC_NSA + 4 * NSA_D]
    win_ref[...] = cols[:, PC_WIN:PC_WIN + 2 * NSA_D]
    s5_ref[...] = cols[:, PC_S5:PC_S5 + S5_COLS]


def _proj_in(x2d, scale, shift, g_mix, w_perm, g_kv, cos_t, sin_t, rows_per_seq, tm):
    r, d = x2d.shape
    if rows_per_seq is not None:
        tps = rows_per_seq // tm
        mod_spec = pl.BlockSpec((None, 1, d), lambda i: (i // tps, 0, 0))
        rope_spec = pl.BlockSpec((tm, MLA_ROPE), lambda i: (i % tps, 0))
    else:
        mod_spec = pl.BlockSpec((tm, d), lambda i: (i, 0))
        rope_spec = pl.BlockSpec((tm, MLA_ROPE), lambda i: (i, 0))
    row = lambda n: pl.BlockSpec((tm, n), lambda i: (i, 0))
    const = lambda a, b: pl.BlockSpec((a, b), lambda i: (0, 0))
    widths = (MLA_QLORA, MLA_ROW, HG_COLS, NSA_H * NSA_D, 4 * NSA_D, 2 * NSA_D, V7X_LANES, S5_COLS)
    return pl.pallas_call(
        _proj_in_kernel,
        out_shape=[jax.ShapeDtypeStruct((r, n), F32) for n in widths],
        grid=(r // tm,),
        in_specs=[row(d), mod_spec, mod_spec, const(1, d), const(d, PC_TOTAL), const(1, MLA_KVLORA),
                  rope_spec, rope_spec],
        out_specs=[row(n) for n in widths],
        compiler_params=_cparams("parallel"),
        name="proj_in",
    )(x2d, scale, shift, g_mix, w_perm, g_kv, cos_t, sin_t)


def _mla_q_kernel(cq_ref, g_ref, wuq_ref, wuk_ref, cos_ref, sin_ref, qlat_ref, qrope_ref):
    cq = cq_ref[...]
    var = jnp.mean(cq * cq, axis=-1, keepdims=True)
    c = cq * lax.rsqrt(var + EPS) * g_ref[...]
    q = _dot(c.astype(BF16), wuq_ref[...])
    scale = (MLA_NOPE + MLA_ROPE) ** -0.5
    n_nope = MLA_H * MLA_NOPE
    n_rope = MLA_H * MLA_ROPE
    qlat_ref[...] = _dot(q[:, :n_nope].astype(BF16), wuk_ref[...]) * scale
    qrope_ref[...] = (q[:, n_nope:n_nope + n_rope] * cos_ref[...]
                      + q[:, n_nope + n_rope:] * sin_ref[...]) * scale


def _mla_q(cq2d, g_q, wuq_perm, wuk_bd, cos_t, sin_t, rows_per_seq, tm):
    r = cq2d.shape[0]
    n_rope = MLA_H * MLA_ROPE
    if rows_per_seq is not None:
        tps = rows_per_seq // tm
        rope_spec = pl.BlockSpec((tm, n_rope), lambda i: (i % tps, 0))
    else:
        rope_spec = pl.BlockSpec((tm, n_rope), lambda i: (i, 0))
    row = lambda n: pl.BlockSpec((tm, n), lambda i: (i, 0))
    const = lambda a, b: pl.BlockSpec((a, b), lambda i: (0, 0))
    n_lat = MLA_H * MLA_KVLORA
    return pl.pallas_call(
        _mla_q_kernel,
        out_shape=[jax.ShapeDtypeStruct((r, n_lat), F32), jax.ShapeDtypeStruct((r, n_rope), F32)],
        grid=(r // tm,),
        in_specs=[row(MLA_QLORA), const(1, MLA_QLORA), const(MLA_QLORA, wuq_perm.shape[1]),
                  const(MLA_H * MLA_NOPE, n_lat), rope_spec, rope_spec],
        out_specs=[row(n_lat), row(n_rope)],
        compiler_params=_cparams("parallel"),
        name="mla_q_prep",
    )(cq2d, g_q, wuq_perm, wuk_bd, cos_t, sin_t)


def _softmax_step(s, mask, m, l, acc, pv):
    if mask is not None:
        s = jnp.where(mask, s, NEG)
    m_new = jnp.maximum(m, jnp.max(s, axis=-1, keepdims=True))
    alpha = jnp.exp(m - m_new)
    p = jnp.exp(s - m_new)
    if mask is not None:
        p = jnp.where(mask, p, 0.0)
    l_new = alpha * l + jnp.sum(p, axis=-1, keepdims=True)
    acc_new = alpha * acc + pv(p.astype(BF16))
    return m_new, l_new, acc_new


def _softmax_init(rows, dv):
    return jnp.full((rows, 1), NEG, F32), jnp.zeros((rows, 1), F32), jnp.zeros((rows, dv), F32)


def _stack_mla_q(qlat, qrope):
    parts = []
    for h in range(MLA_H):
        parts.append(jnp.concatenate([qlat[:, h * MLA_KVLORA:(h + 1) * MLA_KVLORA],
                                      qrope[:, h * MLA_ROPE:(h + 1) * MLA_ROPE]], axis=-1))
    return jnp.concatenate(parts, axis=0).astype(BF16)


def _mla_prompt_kernel(qlat_ref, qrope_ref, keys_ref, o_ref, *, tq, tk):
    i = pl.program_id(1)
    q = _stack_mla_q(qlat_ref[...], qrope_ref[...])
    rows = MLA_H * tq

    def chunk(c, carry, mask):
        start = pl.multiple_of(c * tk, tk)
        kblk = keys_ref[pl.ds(start, tk), :].astype(BF16)
        return _softmax_step(_dot_nt(q, kblk), mask, *carry, lambda p: _dot(p, kblk[:, :MLA_KVLORA]))

    diag = (i * tq) // tk
    carry = lax.fori_loop(0, diag, lambda c, carry: chunk(c, carry, None), _softmax_init(rows, MLA_KVLORA))
    qofs = i * tq - diag * tk + lax.rem(lax.broadcasted_iota(jnp.int32, (rows, tk), 0), tq)
    m, l, acc = chunk(diag, carry, lax.broadcasted_iota(jnp.int32, (rows, tk), 1) <= qofs)
    o = acc / jnp.maximum(l, TINY)
    o_ref[...] = jnp.concatenate([o[h * tq:(h + 1) * tq] for h in range(MLA_H)], axis=-1)


def _mla_prompt(qlat, qrope, keys, tq, tk):
    b, l, _ = qlat.shape
    n_lat = MLA_H * MLA_KVLORA
    assert tk % tq == 0
    return pl.pallas_call(
        functools.partial(_mla_prompt_kernel, tq=tq, tk=tk),
        out_shape=jax.ShapeDtypeStruct((b, l, n_lat), F32),
        grid=(b, l // tq),
        in_specs=[pl.BlockSpec((None, tq, n_lat), lambda bi, i: (bi, i, 0)),
                  pl.BlockSpec((None, tq, MLA_H * MLA_ROPE), lambda bi, i: (bi, i, 0)),
                  pl.BlockSpec((None, l, MLA_ROW), lambda bi, i: (bi, 0, 0))],
        out_specs=pl.BlockSpec((None, tq, n_lat), lambda bi, i: (bi, i, 0)),
        compiler_params=_cparams("parallel", "parallel"),
        name="mla_prompt_attention",
    )(qlat, qrope, keys)


def _hgrn_kernel(hg_ref, lb_ref, gout_ref, ones_ref, s0_ref, o_ref, st_ref,
                 st_scr, qe_scr, ke_scr, v_scr, dl_scr, o_scr, *, bt, tl, c, l_valid, layer):
    j = pl.program_id(1)
    nch = tl // c
    rows = bt * tl
    g = rows // c
    width = HG_H * HG_DK

    @pl.when(j == 0)
    def _():
        st_scr[...] = s0_ref[...]

    x = hg_ref[...].reshape(rows, HG_COLS)
    hq = x[:, 0:width]
    z = x[:, width:2 * width]
    vi = x[:, 2 * width:3 * width]
    hgate = x[:, 3 * width:4 * width]
    raw = lb_ref[...]
    e = jnp.exp(raw - jnp.max(raw, axis=0, keepdims=True))
    lbs = e / jnp.sum(e, axis=0, keepdims=True)
    lb = jnp.zeros((1, width), F32)
    for dd in range(1, layer + 1):
        lb = lb + lbs[dd:dd + 1, :]
    t_in = lax.rem(lax.broadcasted_iota(jnp.int32, (rows, width), 0), c)
    valid = (j * tl + lax.rem(lax.broadcasted_iota(jnp.int32, (rows, width), 0), tl)) < l_valid
    logf = jnp.where(valid, jnp.log(lb + (1.0 - lb) * jax.nn.sigmoid(z)), 0.0)
    kg = jnp.where(valid, (1.0 - lb) * jax.nn.sigmoid(-z), 0.0)
    qg = _silu(hq)
    b = logf
    k = 1
    while k < c:
        b = b + jnp.where(t_in >= k, pltpu.roll(b, k, 0), 0.0)
        k *= 2
    b3 = b.reshape(g, c, width)
    q3 = qg.reshape(g, c, width)
    k3 = kg.reshape(g, c, width)
    v3 = vi.reshape(g, c, width)
    ones_bd = ones_ref[...]
    tt = lax.broadcasted_iota(jnp.int32, (g, c, width), 1)
    o = jnp.zeros((g, c, width), F32)
    for s in range(c):
        bs = b3[:, s:s + 1, :]
        w = q3 * k3[:, s:s + 1, :] * jnp.exp(jnp.where(tt >= s, b3 - bs, NEG))
        r = _dot(w.reshape(rows, width).astype(BF16), ones_bd)
        o = o + r.reshape(g, c, width) * v3[:, s:s + 1, :]
    bl = b3[:, c - 1:c, :]
    o_scr[...] = o.reshape(bt, nch, c, width)
    qe_scr[...] = (q3 * jnp.exp(b3)).reshape(bt, nch, c, width).astype(BF16)
    ke_scr[...] = (k3 * jnp.exp(bl - b3)).reshape(bt, nch, c, width).astype(BF16)
    v_scr[...] = v3.reshape(bt, nch, c, width).astype(BF16)
    dl_scr[...] = jnp.exp(bl).reshape(bt, nch, 1, width)
    bd_mask = ones_bd.astype(F32)

    def chunk(n, carry):
        for bi in range(bt):
            st = st_scr[bi]
            o_scr[bi, n] = o_scr[bi, n] + _dot_nt(qe_scr[bi, n], st.astype(BF16))
            upd = _dot_tn(v_scr[bi, n], ke_scr[bi, n])
            st_scr[bi] = st * dl_scr[bi, n] + upd * bd_mask
        return carry

    lax.fori_loop(0, nch, chunk, 0)
    oo = o_scr[...].reshape(rows, width)
    ms = _dot_split(oo * oo, ones_bd) * (1.0 / HG_DV)
    on = oo * lax.rsqrt(ms + EPS) * gout_ref[...]
    o_ref[...] = (on * _silu(hgate)).reshape(bt, tl, width)

    @pl.when(j == pl.num_programs(1) - 1)
    def _():
        st_ref[...] = st_scr[...]


def _hgrn(hgc, hg_lb, gout_row, ones_bd, st0, bt, tl, c, l_valid, layer):
    b, l, _ = hgc.shape
    width = HG_H * HG_DK
    nch = tl // c
    return pl.pallas_call(
        functools.partial(_hgrn_kernel, bt=bt, tl=tl, c=c, l_valid=l_valid, layer=layer),
        out_shape=[jax.ShapeDtypeStruct((b, l, width), F32), jax.ShapeDtypeStruct((b, width, width), F32)],
        grid=(b // bt, l // tl),
        in_specs=[pl.BlockSpec((bt, tl, HG_COLS), lambda i, j: (i, j, 0)),
                  pl.BlockSpec(hg_lb.shape, lambda i, j: (0, 0)),
                  pl.BlockSpec((1, width), lambda i, j: (0, 0)),
                  pl.BlockSpec((width, width), lambda i, j: (0, 0)),
                  pl.BlockSpec((bt, width, width), lambda i, j: (i, 0, 0))],
        out_specs=[pl.BlockSpec((bt, tl, width), lambda i, j: (i, j, 0)),
                   pl.BlockSpec((bt, width, width), lambda i, j: (i, 0, 0))],
        scratch_shapes=[pltpu.VMEM((bt, width, width), F32),
                        pltpu.VMEM((bt, nch, c, width), BF16),
                        pltpu.VMEM((bt, nch, c, width), BF16),
                        pltpu.VMEM((bt, nch, c, width), BF16),
                        pltpu.VMEM((bt, nch, 1, width), F32),
                        pltpu.VMEM((bt, nch, c, width), F32)],
        compiler_params=_cparams("parallel", "arbitrary"),
        name="hgrn2_chunked",
    )(hgc, hg_lb, gout_row, ones_bd, st0)


def _s5_kernel(u_ref, perm_ref, permt_ref, br_ref, bi_ref, are_ref, aim_ref, ldt_ref, cre_ref, cim_ref, d_ref,
               wglu_ref, bglu_ref, x0r_ref, x0i_ref, y_ref, xr_out, xi_out,
               p_scr, q_scr, xr_scr, xi_scr, cr_scr, ci_scr, *, bt, tl):
    j = pl.program_id(1)
    n = bt * tl

    @pl.when(j == 0)
    def _():
        cr_scr[...] = x0r_ref[...]
        ci_scr[...] = x0i_ref[...]

    u = u_ref[...].reshape(n, S5_COLS)
    u_tm = _dot(perm_ref[...], u.astype(BF16)).astype(BF16)
    p_scr[...] = _dot(u_tm, br_ref[...])
    q_scr[...] = _dot(u_tm, bi_ref[...])
    ar, ai = are_ref[...], aim_ref[...]
    dt = jnp.exp(ldt_ref[...])
    mag = jnp.exp(dt * ar)
    abr1, abi1 = mag * jnp.cos(dt * ai), mag * jnp.sin(dt * ai)
    den = ar * ar + ai * ai
    fr1 = ((abr1 - 1.0) * ar + abi1 * ai) / den
    fi1 = (abi1 * ar - (abr1 - 1.0) * ai) / den
    shape = (bt, S5_STATE)
    abr, abi = jnp.broadcast_to(abr1, shape), jnp.broadcast_to(abi1, shape)
    fr, fi = jnp.broadcast_to(fr1, shape), jnp.broadcast_to(fi1, shape)

    def step(t, carry):
        xr, xi = carry
        off = pl.multiple_of(t * bt, bt)
        p = p_scr[pl.ds(off, bt), :]
        q = q_scr[pl.ds(off, bt), :]
        nxr = abr * xr - abi * xi + (fr * p - fi * q)
        nxi = abr * xi + abi * xr + (fr * q + fi * p)
        xr_scr[pl.ds(off, bt), :] = nxr
        xi_scr[pl.ds(off, bt), :] = nxi
        return nxr, nxi

    xr, xi = lax.fori_loop(0, tl, step, (cr_scr[...], ci_scr[...]))
    cr_scr[...] = xr
    ci_scr[...] = xi
    yc_tm = _dot(xr_scr[...].astype(BF16), cre_ref[...]) - _dot(xi_scr[...].astype(BF16), cim_ref[...])
    hi, lo = _split(yc_tm)
    yc = _dot(permt_ref[...], hi) + _dot(permt_ref[...], lo)
    y = jax.nn.gelu(yc + d_ref[...] * u)
    y = y * jax.nn.sigmoid(_dot(y.astype(BF16), wglu_ref[...]) + bglu_ref[...])
    y_ref[...] = y.reshape(bt, tl, S5_COLS)

    @pl.when(j == pl.num_programs(1) - 1)
    def _():
        xr_out[...] = xr
        xi_out[...] = xi


def _s5(u, perm, permt, br_bd, bi_bd, are_row, aim_row, ldt_row, cre_bd, cim_bd, d_row, wglu, bglu_row, x0r, x0i, bt, tl):
    b, l, _ = u.shape
    n = bt * tl
    const = lambda a, bb: pl.BlockSpec((a, bb), lambda i, j: (0, 0))
    state = pl.BlockSpec((bt, S5_STATE), lambda i, j: (i, 0))
    return pl.pallas_call(
        functools.partial(_s5_kernel, bt=bt, tl=tl),
        out_shape=[jax.ShapeDtypeStruct((b, l, S5_COLS), F32),
                   jax.ShapeDtypeStruct((b, S5_STATE), F32), jax.ShapeDtypeStruct((b, S5_STATE), F32)],
        grid=(b // bt, l // tl),
        in_specs=[pl.BlockSpec((bt, tl, S5_COLS), lambda i, j: (i, j, 0)),
                  const(n, n), const(n, n), const(S5_COLS, S5_STATE), const(S5_COLS, S5_STATE),
                  const(1, S5_STATE), const(1, S5_STATE), const(1, S5_STATE),
                  const(S5_STATE, S5_COLS), const(S5_STATE, S5_COLS), const(1, S5_COLS),
                  const(S5_COLS, S5_COLS), const(1, S5_COLS), state, state],
        out_specs=[pl.BlockSpec((bt, tl, S5_COLS), lambda i, j: (i, j, 0)), state, state],
        scratch_shapes=[pltpu.VMEM((n, S5_STATE), F32), pltpu.VMEM((n, S5_STATE), F32),
                        pltpu.VMEM((n, S5_STATE), F32), pltpu.VMEM((n, S5_STATE), F32),
                        pltpu.VMEM((bt, S5_STATE), F32), pltpu.VMEM((bt, S5_STATE), F32)],
        compiler_params=_cparams("parallel", "arbitrary"),
        name="s5_scan",
    )(u, perm, permt, br_bd, bi_bd, are_row, aim_row, ldt_row, cre_bd, cim_bd, d_row, wglu, bglu_row, x0r, x0i)


def _alibi_rows(rows, t):
    head = lax.broadcasted_iota(jnp.int32, (rows, 1), 0) // t
    slope = jnp.zeros((rows, 1), F32)
    for h in range(NSA_H):
        slope = jnp.where(head == h, 2.0 ** (-8.0 * (h + 1) / NSA_H), slope)
    return slope


def _stack_heads(x, width):
    return jnp.concatenate([x[:, h * width:(h + 1) * width] for h in range(NSA_H)], axis=0)


def _nsa_query_ext(q, slope, sel_bias, tile_bias):
    rows = q.shape[0]
    lane = lax.broadcasted_iota(jnp.int32, (rows, NSA_XW - NSA_X_BLOCKS), 1) + NSA_X_BLOCKS
    tail = jnp.where(lane == NSA_X_HI, float(V7X_LANES) * slope, jnp.where(lane == NSA_X_LO, slope, 0.0))
    if tile_bias is not None:
        tail = jnp.where(lane == NSA_X_ONE, tile_bias, tail)
    if sel_bias is None:
        sel_bias = jnp.zeros((rows, NSA_X_BLOCKS), F32)
    return jnp.concatenate([q, sel_bias, tail], axis=-1).astype(BF16)


def _nsa_compressed(s, mask, pv, t):
    s = jnp.where(mask, s, NEG)
    m = jnp.max(s, axis=-1, keepdims=True)
    e = jnp.where(mask, jnp.exp(s - m), 0.0)
    p = e / jnp.maximum(jnp.sum(e, axis=-1, keepdims=True), TINY)
    psum = p[0:t]
    for h in range(1, NSA_H):
        psum = psum + p[h * t:(h + 1) * t]
    return pv(p.astype(BF16)), psum


def _nsa_rank_select(imp, blk, qpos, n_top, axis):
    imp = jnp.where(blk == qpos // NSA_SEL, jnp.inf, jnp.where(blk * NSA_SEL <= qpos, imp, -jnp.inf))
    rank = jnp.zeros(imp.shape, jnp.int32)
    for c in range(imp.shape[axis]):
        other = imp[c:c + 1, :] if axis == 0 else imp[:, c:c + 1]
        rank = rank + jnp.where(other > imp, 1, jnp.where(other == imp, jnp.where(blk > c, 1, 0), 0))
    return jnp.where(rank < n_top, 1.0, 0.0)


def _nsa_combine(gates, o_c, o_s, o_w, t):
    outs = []
    for h in range(NSA_H):
        lane = PC_GATE_LANE + 3 * h
        rs = slice(h * t, (h + 1) * t)
        outs.append(gates[:, lane:lane + 1] * o_c[rs] + gates[:, lane + 1:lane + 2] * o_s[rs]
                    + gates[:, lane + 2:lane + 3] * o_w[rs])
    return jnp.concatenate(outs, axis=-1)


def _nsa_prompt_kernel(q_ref, rows_ref, win_ref, gate_ref, pairt_ref, kext_ref, eye_ref, o_ref,
                       cm_scr, ks_scr, kw_scr, vs_scr, vw_scr, *, tq, tk, l):
    i = pl.program_id(1)
    ncmp = l // NSA_CMP
    nsel = l // NSA_SEL
    d = NSA_D

    @pl.when(i == 0)
    def _():
        kv = rows_ref[:, 0:2 * d]
        cm_scr[...] = jnp.sum(kv.reshape(ncmp, NSA_CMP, 2 * d), axis=1) * (1.0 / NSA_CMP)
        kext = kext_ref[...]
        ks_scr[...] = jnp.concatenate([rows_ref[:, 2 * d:3 * d].astype(BF16), kext], axis=-1)
        kw_scr[...] = jnp.concatenate([win_ref[:, 0:d].astype(BF16), kext], axis=-1)
        vs_scr[...] = rows_ref[:, 3 * d:4 * d].astype(BF16)
        vw_scr[...] = win_ref[:, d:2 * d].astype(BF16)

    rows = NSA_H * tq
    q = _stack_heads(q_ref[...], d)
    slope = _alibi_rows(rows, tq)
    qofs = lax.rem(lax.broadcasted_iota(jnp.int32, (rows, 1), 0), tq)
    qpos = i * tq + qofs

    cm = cm_scr[...]
    cend = lax.broadcasted_iota(jnp.int32, (rows, ncmp), 1) * NSA_CMP + (NSA_CMP - 1)
    dc = qpos - cend
    s_c = _dot_nt(q.astype(BF16), cm[:, 0:d].astype(BF16)) - slope * dc.astype(F32)
    vcm = cm[:, d:2 * d].astype(BF16)
    o_c, psum = _nsa_compressed(s_c, dc >= 0, lambda p: _dot(p, vcm), tq)

    hi, lo = _split(psum)
    imp_t = _dot_nt(pairt_ref[...], hi) + _dot_nt(pairt_ref[...], lo)
    blk_t = lax.broadcasted_iota(jnp.int32, (nsel, tq), 0)
    qpos_t = i * tq + lax.broadcasted_iota(jnp.int32, (nsel, tq), 1)
    sel_t = _nsa_rank_select(imp_t, blk_t, qpos_t, min(NSA_TOPN, nsel), 0)
    sel = _dot_tn(sel_t.astype(BF16), eye_ref[...])
    sel_bias = jnp.concatenate([(sel - 1.0) * BIAS_BIG] * NSA_H, axis=0)

    q_sel = _nsa_query_ext(q, slope, sel_bias, None)

    def sel_chunk(c, carry, mask):
        start = pl.multiple_of(c * tk, tk)
        s = _dot_nt(q_sel, ks_scr[pl.ds(start, tk), :])
        vblk = vs_scr[pl.ds(start, tk), :]
        return _softmax_step(s, mask, *carry, lambda p: _dot(p, vblk))

    diag = (i * tq) // tk
    carry = lax.fori_loop(0, diag, lambda c, carry: sel_chunk(c, carry, None), _softmax_init(rows, d))
    kofs = lax.broadcasted_iota(jnp.int32, (rows, tk), 1)
    m, lsum, acc = sel_chunk(diag, carry, kofs <= (i * tq - diag * tk) + qofs)
    o_s = acc / jnp.maximum(lsum, TINY)

    q_win = _nsa_query_ext(q, slope, None, None)
    wofs = lax.broadcasted_iota(jnp.int32, (rows, tq), 1)
    n_win = NSA_WINDOW // tq + 1
    carry = _softmax_init(rows, d)
    for jj in range(n_win):
        w = i - (n_win - 1) + jj
        if jj == 0:
            mask = wofs > qofs
        elif jj == n_win - 1:
            mask = wofs <= qofs
        else:
            mask = None

        def win_block(carry, w=w, mask=mask):
            start = pl.multiple_of(w * tq, tq)
            s = _dot_nt(q_win, kw_scr[pl.ds(start, tq), :])
            vblk = vw_scr[pl.ds(start, tq), :]
            return _softmax_step(s, mask, *carry, lambda p: _dot(p, vblk))

        carry = lax.cond(w >= 0, win_block, lambda carry: carry, carry)
    m, lsum, acc = carry
    o_w = acc / jnp.maximum(lsum, TINY)
    o_ref[...] = _nsa_combine(gate_ref[...], o_c, o_s, o_w, tq)


def _nsa_prompt(qn, rows, win, gates, pairt, kext, eye, tq, tk):
    b, l, _ = qn.shape
    d = NSA_D
    assert tk % tq == 0 and NSA_WINDOW % tq == 0 and l // NSA_SEL <= NSA_X_BLOCKS
    seq = lambda n: pl.BlockSpec((None, l, n), lambda bi, i: (bi, 0, 0))
    blk = lambda n: pl.BlockSpec((None, tq, n), lambda bi, i: (bi, i, 0))
    const = lambda a: pl.BlockSpec(a.shape, lambda bi, i: (0, 0))
    return pl.pallas_call(
        functools.partial(_nsa_prompt_kernel, tq=tq, tk=tk, l=l),
        out_shape=jax.ShapeDtypeStruct((b, l, NSA_H * d), F32),
        grid=(b, l // tq),
        in_specs=[blk(NSA_H * d), seq(4 * d), seq(2 * d), blk(V7X_LANES), const(pairt), const(kext), const(eye)],
        out_specs=blk(NSA_H * d),
        scratch_shapes=[pltpu.VMEM((l // NSA_CMP, 2 * d), F32),
                        pltpu.VMEM((l, d + NSA_XW), BF16), pltpu.VMEM((l, d + NSA_XW), BF16),
                        pltpu.VMEM((l, d), BF16), pltpu.VMEM((l, d), BF16)],
        compiler_params=_cparams("parallel", "arbitrary"),
        name="nsa_prompt_attention",
    )(qn, rows, win, gates, pairt, kext, eye)


TPAD = V7X_SUBLANES


def _page_specs(layer, npg, nbt, feat, page, featblk):
    def spec(bi, jj):
        return pl.BlockSpec((None, None, feat, page),
                            lambda b, k, pt: (layer, pt[b * nbt + bi, k * npg + jj], featblk, 0))
    return [spec(bi, jj) for bi in range(nbt) for jj in range(npg)]


def _mla_decode_kernel(pt_ref, qlat_ref, qrope_ref, knew_ref, *rest, npg, nbt, page, l_new):
    page_refs = rest[:nbt * npg]
    o_ref, m_scr, l_scr, acc_scr = rest[nbt * npg:]
    k = pl.program_id(1)
    rows = MLA_H * TPAD

    @pl.when(k == 0)
    def _():
        m_scr[...] = jnp.full(m_scr.shape, NEG, F32)
        l_scr[...] = jnp.zeros(l_scr.shape, F32)
        acc_scr[...] = jnp.zeros(acc_scr.shape, F32)

    qs = [_stack_mla_q(qlat_ref[bi], qrope_ref[bi]) for bi in range(nbt)]
    for bi in range(nbt):
        pages = [page_refs[bi * npg + jj][...].astype(BF16) for jj in range(npg)]
        s = jnp.concatenate([_dot(qs[bi], pg) for pg in pages], axis=-1)

        def pv(p, pages=pages):
            out = _dot_nt(p[:, 0:page], pages[0][0:MLA_KVLORA, :])
            for jj in range(1, npg):
                out = out + _dot_nt(p[:, jj * page:(jj + 1) * page], pages[jj][0:MLA_KVLORA, :])
            return out

        m_scr[bi], l_scr[bi], acc_scr[bi] = _softmax_step(s, None, m_scr[bi], l_scr[bi], acc_scr[bi], pv)

    @pl.when(k == pl.num_programs(1) - 1)
    def _():
        tq = lax.rem(lax.broadcasted_iota(jnp.int32, (rows, TPAD), 0), TPAD)
        tk = lax.broadcasted_iota(jnp.int32, (rows, TPAD), 1)
        mask = (tk <= tq) & (tk < l_new)
        for bi in range(nbt):
            knew = knew_ref[bi].astype(BF16)
            mm, ll, aa = _softmax_step(_dot_nt(qs[bi], knew), mask, m_scr[bi], l_scr[bi], acc_scr[bi],
                                       lambda p, knew=knew: _dot(p, knew[:, :MLA_KVLORA]))
            o = aa / jnp.maximum(ll, TINY)
            o_ref[bi] = jnp.concatenate([o[h * TPAD:(h + 1) * TPAD] for h in range(MLA_H)], axis=-1)


def _mla_decode(page_table, qlat, qrope, knew, cache_t, layer, npg, nbt, l_new):
    nb, n_pages = page_table.shape
    page = cache_t.shape[3]
    n_lat = MLA_H * MLA_KVLORA
    rows = MLA_H * TPAD
    per_b = lambda n: pl.BlockSpec((nbt, TPAD, n), lambda b, k, pt: (b, 0, 0))
    grid_spec = pltpu.PrefetchScalarGridSpec(
        num_scalar_prefetch=1,
        grid=(nb // nbt, n_pages // npg),
        in_specs=[per_b(n_lat), per_b(MLA_H * MLA_ROPE), per_b(MLA_ROW)]
                 + _page_specs(layer, npg, nbt, MLA_ROW, page, 0),
        out_specs=per_b(n_lat),
        scratch_shapes=[pltpu.VMEM((nbt, rows, 1), F32), pltpu.VMEM((nbt, rows, 1), F32),
                        pltpu.VMEM((nbt, rows, MLA_KVLORA), F32)])
    return pl.pallas_call(
        functools.partial(_mla_decode_kernel, npg=npg, nbt=nbt, page=page, l_new=l_new),
        out_shape=jax.ShapeDtypeStruct((nb, TPAD, n_lat), F32),
        grid_spec=grid_spec,
        compiler_params=_cparams("parallel", "arbitrary"),
        name="mla_paged_decode",
    )(page_table, qlat, qrope, knew, *([cache_t] * (nbt * npg)))


def _nsa_cmp_decode_kernel(pt_ref, q_ref, pool_ref, pair_ref, *rest, npg, nbt, page, past_len):
    page_refs = rest[:nbt * npg]
    oc_ref, sel_ref, cm_scr = rest[nbt * npg:]
    k = pl.program_id(1)
    d = NSA_D
    n_steps = cm_scr.shape[1]
    for bi in range(nbt):
        acc = jnp.zeros((2 * d, cm_scr.shape[3]), F32)
        for jj in range(npg):
            hi, lo = _split(page_refs[bi * npg + jj][...])
            acc = acc + _dot(hi, pool_ref[jj]) + _dot(lo, pool_ref[jj])
        cm_scr[bi, k] = acc

    @pl.when(k == pl.num_programs(1) - 1)
    def _():
        rows = NSA_H * TPAD
        slope = _alibi_rows(rows, TPAD)
        qpos = past_len + lax.rem(lax.broadcasted_iota(jnp.int32, (rows, 1), 0), TPAD)
        ncmp = past_len // NSA_CMP
        nsel = past_len // NSA_SEL
        cend = lax.broadcasted_iota(jnp.int32, (rows, ncmp), 1) * NSA_CMP + (NSA_CMP - 1)
        dc = qpos - cend
        blk = lax.broadcasted_iota(jnp.int32, (TPAD, nsel), 1)
        for bi in range(nbt):
            q = _stack_heads(q_ref[bi], d).astype(BF16)
            cm_t = jnp.concatenate([cm_scr[bi, kk] for kk in range(n_steps)], axis=-1)
            s_c = _dot(q, cm_t[0:d].astype(BF16)) - slope * dc.astype(F32)
            vcm_t = cm_t[d:2 * d].astype(BF16)
            o_c, psum = _nsa_compressed(s_c, dc >= 0, lambda p, vcm_t=vcm_t: _dot_nt(p, vcm_t), TPAD)
            oc_ref[bi] = o_c
            sel_ref[bi] = _nsa_rank_select(_dot_split(psum, pair_ref[...]), blk, qpos[0:TPAD], NSA_TOPN - 1, 1)


def _nsa_cmp_decode(page_table, qn, pool, pair, cache_t, layer, npg, nbt, past_len):
    nb, n_pages = page_table.shape
    page = cache_t.shape[3]
    d = NSA_D
    nsel = past_len // NSA_SEL
    rows = NSA_H * TPAD
    n_steps = n_pages // npg
    grid_spec = pltpu.PrefetchScalarGridSpec(
        num_scalar_prefetch=1,
        grid=(nb // nbt, n_steps),
        in_specs=[pl.BlockSpec((nbt, TPAD, NSA_H * d), lambda b, k, pt: (b, 0, 0)),
                  pl.BlockSpec(pool.shape, lambda b, k, pt: (0, 0, 0)),
                  pl.BlockSpec(pair.shape, lambda b, k, pt: (0, 0))]
                 + _page_specs(layer, npg, nbt, 2 * d, page, 0),
        out_specs=[pl.BlockSpec((nbt, rows, d), lambda b, k, pt: (b, 0, 0)),
                   pl.BlockSpec((nbt, TPAD, nsel), lambda b, k, pt: (b, 0, 0))],
        scratch_shapes=[pltpu.VMEM((nbt, n_steps, 2 * d, npg * page // NSA_CMP), F32)])
    return pl.pallas_call(
        functools.partial(_nsa_cmp_decode_kernel, npg=npg, nbt=nbt, page=page, past_len=past_len),
        out_shape=[jax.ShapeDtypeStruct((nb, rows, d), F32), jax.ShapeDtypeStruct((nb, TPAD, nsel), F32)],
        grid_spec=grid_spec,
        compiler_params=_cparams("parallel", "arbitrary"),
        name="nsa_compressed_decode",
    )(page_table, qn, pool, pair, *([cache_t] * (nbt * npg)))


def _nsa_sel_decode_kernel(pt_ref, q_ref, sel_ref, kext_ref, oc_ref, rnew_ref, wpast_ref, wnew_ref, gate_ref, *rest,
                           npg, nbt, page, past_len, l_new):
    page_refs = rest[:nbt * npg]
    o_ref, m_scr, l_scr, acc_scr = rest[nbt * npg:]
    k = pl.program_id(1)
    d = NSA_D
    rows = NSA_H * TPAD

    @pl.when(k == 0)
    def _():
        m_scr[...] = jnp.full(m_scr.shape, NEG, F32)
        l_scr[...] = jnp.zeros(l_scr.shape, F32)
        acc_scr[...] = jnp.zeros(acc_scr.shape, F32)

    slope = _alibi_rows(rows, TPAD)
    tq = lax.rem(lax.broadcasted_iota(jnp.int32, (rows, 1), 0), TPAD)
    tile_bias = slope * (k * (npg * page)).astype(F32)
    qs = [_stack_heads(q_ref[bi], d) for bi in range(nbt)]
    for bi in range(nbt):
        sel_bias = jnp.concatenate([(sel_ref[bi] - 1.0) * BIAS_BIG] * NSA_H, axis=0)
        q_ext = _nsa_query_ext(qs[bi], slope, sel_bias, tile_bias)
        pages = [page_refs[bi * npg + jj][...].astype(BF16) for jj in range(npg)]
        s = jnp.concatenate([_dot(q_ext, jnp.concatenate([pages[jj][0:d, :], kext_ref[jj]], axis=0))
                             for jj in range(npg)], axis=-1)

        def pv(p, pages=pages):
            out = _dot_nt(p[:, 0:page], pages[0][d:2 * d, :])
            for jj in range(1, npg):
                out = out + _dot_nt(p[:, jj * page:(jj + 1) * page], pages[jj][d:2 * d, :])
            return out

        m_scr[bi], l_scr[bi], acc_scr[bi] = _softmax_step(s, None, m_scr[bi], l_scr[bi], acc_scr[bi], pv)

    @pl.when(k == pl.num_programs(1) - 1)
    def _():
        tk = lax.broadcasted_iota(jnp.int32, (rows, TPAD), 1)
        new_mask = (tk <= tq) & (tk < l_new)
        new_bias = slope * (tq - tk).astype(F32)
        for bi in range(nbt):
            q = qs[bi].astype(BF16)
            rnew = rnew_ref[bi].astype(BF16)
            sn = _dot_nt(q, rnew[:, 2 * d:3 * d]) + slope * (past_len + tk).astype(F32)
            mm, ll, aa = _softmax_step(sn, new_mask, m_scr[bi], l_scr[bi], acc_scr[bi],
                                       lambda p, rnew=rnew: _dot(p, rnew[:, 3 * d:4 * d]))
            o_s = aa / jnp.maximum(ll, TINY)
            wpast = wpast_ref[bi].astype(BF16)
            wlen = wpast.shape[1]
            dw = tq + (wlen - lax.broadcasted_iota(jnp.int32, (rows, wlen), 1))
            sw = _dot(q, wpast[0:d, :]) - slope * dw.astype(F32)
            mm, ll, aa = _softmax_step(sw, dw < NSA_WINDOW, *_softmax_init(rows, d),
                                       lambda p, wpast=wpast: _dot_nt(p, wpast[d:2 * d, :]))
            wnew = wnew_ref[bi].astype(BF16)
            swn = _dot_nt(q, wnew[:, 0:d]) - new_bias
            mm, ll, aa = _softmax_step(swn, new_mask, mm, ll, aa, lambda p, wnew=wnew: _dot(p, wnew[:, d:2 * d]))
            o_w = aa / jnp.maximum(ll, TINY)
            o_ref[bi] = _nsa_combine(gate_ref[bi], oc_ref[bi], o_s, o_w, TPAD)


def _nsa_sel_decode(page_table, qn, sel_steps, kext, o_c, rows_new, win_past_t, win_new, gates, cache_t, layer, npg,
                    nbt, past_len, l_new):
    nb, n_pages = page_table.shape
    page = cache_t.shape[3]
    d = NSA_D
    rows = NSA_H * TPAD
    wlen = win_past_t.shape[2]
    per_b = lambda a, n: pl.BlockSpec((nbt, a, n), lambda b, k, pt: (b, 0, 0))
    grid_spec = pltpu.PrefetchScalarGridSpec(
        num_scalar_prefetch=1,
        grid=(nb // nbt, n_pages // npg),
        in_specs=[per_b(TPAD, NSA_H * d),
                  pl.BlockSpec((nbt, None, TPAD, sel_steps.shape[3]), lambda b, k, pt: (b, k, 0, 0)),
                  pl.BlockSpec(kext.shape, lambda b, k, pt: (0, 0, 0)),
                  per_b(rows, d), per_b(TPAD, 4 * d), per_b(2 * d, wlen), per_b(TPAD, 2 * d), per_b(TPAD, V7X_LANES)]
                 + _page_specs(layer, npg, nbt, 2 * d, page, 1),
        out_specs=per_b(TPAD, NSA_H * d),
        scratch_shapes=[pltpu.VMEM((nbt, rows, 1), F32), pltpu.VMEM((nbt, rows, 1), F32),
                        pltpu.VMEM((nbt, rows, d), F32)])
    return pl.pallas_call(
        functools.partial(_nsa_sel_decode_kernel, npg=npg, nbt=nbt, page=page, past_len=past_len, l_new=l_new),
        out_shape=jax.ShapeDtypeStruct((nb, TPAD, NSA_H * d), F32),
        grid_spec=grid_spec,
        compiler_params=_cparams("parallel", "arbitrary"),
        name="nsa_selected_window_decode",
    )(page_table, qn, sel_steps, kext, o_c, rows_new, win_past_t, win_new, gates, *([cache_t] * (nbt * npg)))


def _out_proj_kernel(x_ref, olat_ref, ohg_ref, onsa_ref, os5_ref, wuv_ref, wout_ref, g1_ref, sc2_ref, sh2_ref,
                     gffn_ref, x1_ref, h2_ref):
    w = NSA_H * NSA_D
    omla = _dot(olat_ref[...].astype(BF16), wuv_ref[...])
    mix = (_dot(omla.astype(BF16), wout_ref[0:w, :]) + _dot(ohg_ref[...].astype(BF16), wout_ref[w:2 * w, :])
           + _dot(onsa_ref[...].astype(BF16), wout_ref[2 * w:3 * w, :])
           + _dot(os5_ref[...].astype(BF16), wout_ref[3 * w:4 * w, :]))
    x1 = x_ref[...] + g1_ref[...] * mix
    x1_ref[...] = x1
    var = jnp.mean(x1 * x1, axis=-1, keepdims=True)
    h2 = (x1 * lax.rsqrt(var + EPS) * gffn_ref[...]) * (1.0 + sc2_ref[...]) + sh2_ref[...]
    h2_ref[...] = h2.astype(BF16)


def _out_proj(x2d, olat, ohg, onsa, os5, wuv_bd, wout, g1, sc2, sh2, g_ffn, rows_per_seq, tm):
    r, d = x2d.shape
    if rows_per_seq is not None:
        tps = rows_per_seq // tm
        mod_spec = pl.BlockSpec((None, 1, d), lambda i: (i // tps, 0, 0))
    else:
        mod_spec = pl.BlockSpec((tm, d), lambda i: (i, 0))
    row = lambda n: pl.BlockSpec((tm, n), lambda i: (i, 0))
    const = lambda a, b: pl.BlockSpec((a, b), lambda i: (0, 0))
    w = NSA_H * NSA_D
    return pl.pallas_call(
        _out_proj_kernel,
        out_shape=[jax.ShapeDtypeStruct((r, d), F32), jax.ShapeDtypeStruct((r, d), BF16)],
        grid=(r // tm,),
        in_specs=[row(d), row(MLA_H * MLA_KVLORA), row(w), row(w), row(w), const(MLA_H * MLA_KVLORA, w),
                  const(4 * w, d), mod_spec, mod_spec, mod_spec, const(1, d)],
        out_specs=[row(d), row(d)],
        compiler_params=_cparams("parallel"),
        name="out_proj_residual",
    )(x2d, olat, ohg, onsa, os5, wuv_bd, wout, g1, sc2, sh2, g_ffn)


def _ffn_tail(j, upg_scr, upv_scr, offs, tm, cwg_ref, cwv_ref, cbg_ref, cbv_ref, wd_ref, x1_ref, g2_ref, o_ref,
              acc_scr):
    g = cbg_ref[...]
    v = cbv_ref[...]
    for k in range(CONV_W):
        g = g + upg_scr[pl.ds(offs[k], tm), :] * cwg_ref[k:k + 1, :]
        v = v + upv_scr[pl.ds(offs[k], tm), :] * cwv_ref[k:k + 1, :]
    part = _dot((_silu(g) * v).astype(BF16), wd_ref[...])

    @pl.when(j == 0)
    def _():
        acc_scr[...] = part

    @pl.when(j > 0)
    def _():
        acc_scr[...] = acc_scr[...] + part

    @pl.when(j == pl.num_programs(1) - 1)
    def _():
        o_ref[...] = x1_ref[...] + g2_ref[...] * acc_scr[...]


def _ffn_seq_kernel(h2_ref, halo_ref, x1_ref, g2_ref, wg_ref, wv_ref, cwg_ref, cwv_ref, cbg_ref, cbv_ref, wd_ref,
                    o_ref, sg_ref, sv_ref, upg_scr, upv_scr, acc_scr, *, tm, tiles_per_seq, halo):
    i, j = pl.program_id(0), pl.program_id(1)
    he = jnp.concatenate([halo_ref[...], h2_ref[...]], axis=0)
    rowid = lax.broadcasted_iota(jnp.int32, (halo + tm, 1), 0)
    keep = jnp.logical_or(rowid >= halo, lax.rem(i, tiles_per_seq) != 0)
    upg_scr[...] = jnp.where(keep, _dot(he, wg_ref[...]), 0.0)
    upv_scr[...] = jnp.where(keep, _dot(he, wv_ref[...]), 0.0)
    sg_ref[...] = upg_scr[pl.ds(halo + tm - (CONV_W - 1), CONV_W - 1), :]
    sv_ref[...] = upv_scr[pl.ds(halo + tm - (CONV_W - 1), CONV_W - 1), :]
    offs = tuple(halo - (CONV_W - 1) + k for k in range(CONV_W))
    _ffn_tail(j, upg_scr, upv_scr, offs, tm, cwg_ref, cwv_ref, cbg_ref, cbv_ref, wd_ref, x1_ref, g2_ref, o_ref,
              acc_scr)


def _ffn_seq(h2, x1, g2, wg, wv, cwg, cwv, cbg, cbv, wd, rows_per_seq, tm, ck):
    r, d = x1.shape
    dff = wd.shape[0]
    nj = dff // ck
    tps = rows_per_seq // tm
    halo = 16
    hb = tm // halo
    xo, sg, sv = pl.pallas_call(
        functools.partial(_ffn_seq_kernel, tm=tm, tiles_per_seq=tps, halo=halo),
        out_shape=[jax.ShapeDtypeStruct((r, d), F32),
                   jax.ShapeDtypeStruct((r // tm, CONV_W - 1, dff), F32),
                   jax.ShapeDtypeStruct((r // tm, CONV_W - 1, dff), F32)],
        grid=(r // tm, nj),
        in_specs=[pl.BlockSpec((tm, d), lambda i, j: (i, 0)),
                  pl.BlockSpec((halo, d), lambda i, j: (jnp.maximum(i * hb - 1, 0), 0)),
                  pl.BlockSpec((tm, d), lambda i, j: (i, 0)),
                  pl.BlockSpec((None, 1, d), lambda i, j: (i // tps, 0, 0)),
                  pl.BlockSpec((d, ck), lambda i, j: (0, j)),
                  pl.BlockSpec((d, ck), lambda i, j: (0, j)),
                  pl.BlockSpec((CONV_W, ck), lambda i, j: (0, j)),
                  pl.BlockSpec((CONV_W, ck), lambda i, j: (0, j)),
                  pl.BlockSpec((1, ck), lambda i, j: (0, j)),
                  pl.BlockSpec((1, ck), lambda i, j: (0, j)),
                  pl.BlockSpec((ck, d), lambda i, j: (j, 0))],
        out_specs=[pl.BlockSpec((tm, d), lambda i, j: (i, 0)),
                   pl.BlockSpec((None, CONV_W - 1, ck), lambda i, j: (i, 0, j)),
                   pl.BlockSpec((None, CONV_W - 1, ck), lambda i, j: (i, 0, j))],
        scratch_shapes=[pltpu.VMEM((halo + tm, ck), F32), pltpu.VMEM((halo + tm, ck), F32),
                        pltpu.VMEM((tm, d), F32)],
        compiler_params=_cparams("parallel", "arbitrary"),
        name="ffn_conv_prompt",
    )(h2, h2, x1, g2, wg, wv, cwg, cwv, cbg, cbv, wd)
    return xo, sg[tps - 1::tps], sv[tps - 1::tps]


def _ffn_step_kernel(h2_ref, sg_in_ref, sv_in_ref, x1_ref, g2_ref, wg_ref, wv_ref, cwg_ref, cwv_ref, cbg_ref, cbv_ref,
                     wd_ref, o_ref, sg_ref, sv_ref, upg_scr, upv_scr, acc_scr, *, tm, nb):
    j = pl.program_id(1)
    hist = (CONV_W - 1) * nb
    upg_scr[0:hist, :] = sg_in_ref[...]
    upv_scr[0:hist, :] = sv_in_ref[...]
    upg_scr[hist:hist + tm, :] = _dot(h2_ref[...], wg_ref[...])
    upv_scr[hist:hist + tm, :] = _dot(h2_ref[...], wv_ref[...])
    sg_ref[...] = upg_scr[tm:tm + hist, :]
    sv_ref[...] = upv_scr[tm:tm + hist, :]
    offs = tuple(k * nb for k in range(CONV_W))
    _ffn_tail(j, upg_scr, upv_scr, offs, tm, cwg_ref, cwv_ref, cbg_ref, cbv_ref, wd_ref, x1_ref, g2_ref, o_ref,
              acc_scr)


def _ffn_step(h2, sg_in, sv_in, x1, g2, wg, wv, cwg, cwv, cbg, cbv, wd, nb, ck):
    tm, d = x1.shape
    dff = wd.shape[0]
    nj = dff // ck
    hist = (CONV_W - 1) * nb
    full = lambda a, b: pl.BlockSpec((a, b), lambda i, j: (0, 0))
    colblk = lambda a: pl.BlockSpec((a, ck), lambda i, j: (0, j))
    return pl.pallas_call(
        functools.partial(_ffn_step_kernel, tm=tm, nb=nb),
        out_shape=[jax.ShapeDtypeStruct((tm, d), F32), jax.ShapeDtypeStruct((hist, dff), F32),
                   jax.ShapeDtypeStruct((hist, dff), F32)],
        grid=(1, nj),
        in_specs=[full(tm, d), colblk(hist), colblk(hist), full(tm, d), full(tm, d), colblk(d), colblk(d),
                  colblk(CONV_W), colblk(CONV_W), colblk(1), colblk(1),
                  pl.BlockSpec((ck, d), lambda i, j: (j, 0))],
        out_specs=[full(tm, d), colblk(hist), colblk(hist)],
        scratch_shapes=[pltpu.VMEM((hist + tm, ck), F32), pltpu.VMEM((hist + tm, ck), F32),
                        pltpu.VMEM((tm, d), F32)],
        compiler_params=_cparams("parallel", "arbitrary"),
        name="ffn_conv_step",
    )(h2, sg_in, sv_in, x1, g2, wg, wv, cwg, cwv, cbg, cbv, wd)


def _final_norm_kernel(x_ref, g_ref, o_ref):
    x = x_ref[...]
    var = jnp.mean(x * x, axis=-1, keepdims=True)
    o_ref[...] = x * lax.rsqrt(var + EPS) * g_ref[...]


def _final_norm(x2d, g, tm):
    r, d = x2d.shape
    return pl.pallas_call(
        _final_norm_kernel,
        out_shape=jax.ShapeDtypeStruct((r, d), F32),
        grid=(r // tm,),
        in_specs=[pl.BlockSpec((tm, d), lambda i: (i, 0)), pl.BlockSpec((1, d), lambda i: (0, 0))],
        out_specs=pl.BlockSpec((tm, d), lambda i: (i, 0)),
        compiler_params=_cparams("parallel"),
        name="final_rmsnorm",
    )(x2d, g)


def _rope_tables(pos):
    half = MLA_ROPE // 2
    inv = ROPE_BASE ** (-jnp.arange(half, dtype=F32) / half)
    ang = pos.astype(F32)[:, None] * inv[None, :]
    c, s = jnp.cos(ang), jnp.sin(ang)
    return jnp.concatenate([c, c], axis=-1), jnp.concatenate([-s, s], axis=-1)


def _swap_halves(w):
    half = w.shape[-1] // 2
    return jnp.concatenate([w[..., half:], w[..., :half]], axis=-1)


def _block_diag(blocks):
    g, a, b = blocks.shape
    eye = jnp.eye(g, dtype=blocks.dtype)
    return (blocks[:, :, None, :] * eye[:, None, :, None]).reshape(g * a, g * b)


def _time_major_perm(bt, tl):
    n = bt * tl
    src = jnp.arange(n)
    dst = (src % tl) * bt + src // tl
    return (dst[:, None] == jnp.arange(n)[None, :]).astype(BF16).T


def _nsa_key_ext(pos, block):
    lane = jnp.arange(NSA_XW)[None, :]
    onehot = (lane == block[:, None]) & (lane < NSA_X_BLOCKS)
    ext = (onehot.astype(F32) + jnp.where(lane == NSA_X_HI, (pos // V7X_LANES)[:, None], 0)
           + jnp.where(lane == NSA_X_LO, (pos % V7X_LANES)[:, None], 0) + (lane == NSA_X_ONE))
    return ext.astype(BF16)


def _layer_consts(l, w_in, w_out, mla_w_uq, mla_w_uk, mla_w_uv, s5_b_re, s5_b_im, s5_c_re, s5_c_im, ffn_w_up,
                  ffn_w_down):
    d = w_in.shape[1]
    w = w_in[l]
    o = 0
    cq = w[:, o:o + MLA_QLORA]; o += MLA_QLORA
    ckv = w[:, o:o + MLA_KVLORA]; o += MLA_KVLORA
    kr = w[:, o:o + MLA_ROPE]; o += MLA_ROPE
    hgw = w[:, o:o + HG_COLS]; o += HG_COLS
    qn = w[:, o:o + NSA_H * NSA_D]; o += NSA_H * NSA_D
    kvs = w[:, o:o + 6 * NSA_D]; o += 6 * NSA_D
    gates = w[:, o:o + 3 * NSA_H]; o += 3 * NSA_H
    s5w = w[:, o:o + S5_COLS]
    pad = jnp.zeros((d, V7X_LANES - 2 * MLA_ROPE - 3 * NSA_H), w.dtype)
    w_perm = jnp.concatenate([cq, ckv, kr, _swap_halves(kr), gates, pad, hgw, qn, kvs, s5w], axis=1).astype(BF16)
    uq = mla_w_uq[l].reshape(MLA_QLORA, MLA_H, MLA_NOPE + MLA_ROPE)
    uq_rope = uq[:, :, MLA_NOPE:]
    wuq_perm = jnp.concatenate([uq[:, :, :MLA_NOPE].reshape(MLA_QLORA, -1), uq_rope.reshape(MLA_QLORA, -1),
                                _swap_halves(uq_rope).reshape(MLA_QLORA, -1)], axis=1).astype(BF16)
    return dict(
        w_perm=w_perm, wuq_perm=wuq_perm,
        wuk_bd=_block_diag(mla_w_uk[l].transpose(1, 2, 0)).astype(BF16),
        wuv_bd=_block_diag(mla_w_uv[l].transpose(1, 0, 2)).astype(BF16),
        wout=w_out[l].astype(BF16),
        br_bd=_block_diag(s5_b_re[l].transpose(0, 2, 1)).astype(BF16),
        bi_bd=_block_diag(s5_b_im[l].transpose(0, 2, 1)).astype(BF16),
        cre_bd=_block_diag(s5_c_re[l].transpose(0, 2, 1)).astype(BF16),
        cim_bd=_block_diag(s5_c_im[l].transpose(0, 2, 1)).astype(BF16),
        wg=ffn_w_up[l][:, :ffn_w_down.shape[1]].astype(BF16),
        wv=ffn_w_up[l][:, ffn_w_down.shape[1]:].astype(BF16),
        wd=ffn_w_down[l].astype(BF16),
    )


def _hgrn_state_to_bd(s):
    b = s.shape[0]
    blocks = s.transpose(0, 1, 3, 2)
    eye = jnp.eye(HG_H, dtype=s.dtype)
    return (blocks[:, :, :, None, :] * eye[None, :, None, :, None]).reshape(b, HG_H * HG_DV, HG_H * HG_DK)


def _hgrn_state_from_bd(st):
    b = st.shape[0]
    st5 = st.reshape(b, HG_H, HG_DV, HG_H, HG_DK)
    return jnp.stack([st5[:, h, :, h, :] for h in range(HG_H)], axis=1).transpose(0, 1, 3, 2)


def _pad_t(x):
    return jnp.pad(x, ((0, 0), (0, TPAD - x.shape[1]), (0, 0)))


def kernel(x_prompt, x_sample, c_prompt, c_sample, cache_mla, cache_nsa, cache_nsa_win, state_hgrn, state_s5_re, state_s5_im, state_ffn_conv, page_table, w_ada, b_ada, g_mix, g_ffn, w_in, w_out, mla_g_q, mla_g_kv, mla_w_uq, mla_w_uk, mla_w_uv, hg_lb, hg_g_out, s5_a_re, s5_a_im, s5_log_dt, s5_b_re, s5_b_im, s5_c_re, s5_c_im, s5_d, s5_w_glu, s5_b_glu, ffn_w_up, ffn_conv_w, ffn_conv_b, ffn_w_down, g_final):
    depth = w_in.shape[0]
    bp, lp, d = x_prompt.shape
    nb, ls, _ = x_sample.shape
    page = cache_mla.shape[2]
    n_pages = page_table.shape[1]
    past_len = n_pages * page
    dff = ffn_w_down.shape[1]
    rp, rs = bp * lp, nb * ls
    assert ls <= TPAD and page == V7X_LANES and lp % NSA_SEL == 0
    assert cache_nsa_win.shape[2] == NSA_WINDOW and past_len >= NSA_WINDOW

    tm_in, tm_row, tq_mla, tk_mla, tq_nsa, tk_nsa = 256, 512, 256, 256, 128, 512
    bt, tl_rec, c_hg = 8, 128, 16
    ck = 896
    npg, nbt = 16, 4
    npg_cmp, nbt_cmp = 32, 2
    assert (npg * page // NSA_SEL) == NSA_X_BLOCKS and npg_cmp * page // NSA_CMP == V7X_LANES

    ones_bd = _block_diag(jnp.ones((HG_H, HG_DK, HG_DV), BF16))
    cos_p, sin_p = _rope_tables(jnp.arange(lp))
    cos_s, sin_s = _rope_tables(past_len + jnp.arange(ls))
    cos_s, sin_s = jnp.tile(cos_s, (nb, 1)), jnp.tile(sin_s, (nb, 1))
    tile_h = lambda t: jnp.tile(t, (1, MLA_H))
    perm_p = _time_major_perm(bt, tl_rec)
    perm_s = _time_major_perm(bt, ls)
    ncmp_p, nsel_p = lp // NSA_CMP, lp // NSA_SEL
    cmp_per_sel = NSA_SEL // NSA_CMP
    pairt_p = (jnp.arange(nsel_p)[:, None] == jnp.arange(ncmp_p)[None, :] // cmp_per_sel).astype(BF16)
    kext_p = _nsa_key_ext(jnp.arange(lp), jnp.arange(lp) // NSA_SEL)
    eye_p = jnp.eye(nsel_p, NSA_X_BLOCKS, dtype=BF16)
    ncmp_s, nsel_s = past_len // NSA_CMP, past_len // NSA_SEL
    pair_s = (jnp.arange(ncmp_s)[:, None] // cmp_per_sel == jnp.arange(nsel_s)[None, :]).astype(BF16)
    tok = jnp.arange(npg * page)
    kext_s = _nsa_key_ext(tok, tok // NSA_SEL).reshape(npg, page, NSA_XW).transpose(0, 2, 1)
    ptok = jnp.arange(npg_cmp * page).reshape(npg_cmp, page)
    pool = ((ptok[:, :, None] // NSA_CMP == jnp.arange(V7X_LANES)[None, None, :]).astype(F32)
            * (1.0 / NSA_CMP)).astype(BF16)
    cache_mla_t = cache_mla.transpose(0, 1, 3, 2)
    cache_nsa_t = cache_nsa.transpose(0, 1, 3, 4, 2).reshape(cache_nsa.shape[0], cache_nsa.shape[1], 4 * NSA_D, page)
    win_t = cache_nsa_win.transpose(0, 1, 3, 4, 2).reshape(depth, nb, 2 * NSA_D, NSA_WINDOW)

    mod = _modulation(jnp.concatenate([c_prompt, c_sample], axis=0), w_ada, b_ada)
    xp = x_prompt.reshape(rp, d)
    xs = x_sample.reshape(rs, d)
    leaves_p, leaves_s = [], []
    row1 = lambda v: v.reshape(1, -1)
    for l in range(depth):
        cst = _layer_consts(l, w_in, w_out, mla_w_uq, mla_w_uk, mla_w_uv, s5_b_re, s5_b_im, s5_c_re, s5_c_im,
                            ffn_w_up, ffn_w_down)
        gout_row = row1(jnp.tile(hg_g_out[l], HG_H))
        are_row, aim_row = row1(s5_a_re[l]), row1(s5_a_im[l])
        ldt_row = row1(jnp.repeat(s5_log_dt[l], S5_N))
        d_row, bglu_row = row1(s5_d[l]), row1(s5_b_glu[l])
        wglu = s5_w_glu[l].astype(BF16)
        cwg, cwv = ffn_conv_w[l][:, :dff], ffn_conv_w[l][:, dff:]
        cbg, cbv = row1(ffn_conv_b[l][:dff]), row1(ffn_conv_b[l][dff:])
        s5_args = (cst["br_bd"], cst["bi_bd"], are_row, aim_row, ldt_row, cst["cre_bd"], cst["cim_bd"], d_row, wglu,
                   bglu_row)

        m6 = mod[l, :bp].reshape(bp, 6, 1, d)
        sh1, sc1, g1, sh2, sc2, g2 = (m6[:, i] for i in range(6))
        cq, mla_rows, hgc, qn, nsa_rows, win, gates, s5u = _proj_in(
            xp, sc1, sh1, row1(g_mix[l]), cst["w_perm"], row1(mla_g_kv[l]), cos_p, sin_p, lp, tm_in)
        qlat, qrope = _mla_q(cq, row1(mla_g_q[l]), cst["wuq_perm"], cst["wuk_bd"], tile_h(cos_p), tile_h(sin_p), lp,
                             tm_row)
        olat = _mla_prompt(qlat.reshape(bp, lp, -1), qrope.reshape(bp, lp, -1), mla_rows.reshape(bp, lp, -1),
                           tq_mla, tk_mla)
        ohg, st = _hgrn(hgc.reshape(bp, lp, -1), hg_lb, gout_row, ones_bd,
                        jnp.zeros((bp, HG_H * HG_DV, HG_H * HG_DK), F32), bt, tl_rec, c_hg, lp, l)
        onsa = _nsa_prompt(qn.reshape(bp, lp, -1), nsa_rows.reshape(bp, lp, -1), win.reshape(bp, lp, -1),
                           gates.reshape(bp, lp, -1), pairt_p, kext_p, eye_p, tq_nsa, tk_nsa)
        zero_state = jnp.zeros((bp, S5_STATE), F32)
        os5, s5r, s5i = _s5(s5u.reshape(bp, lp, -1), perm_p, perm_p.T, *s5_args, zero_state, zero_state, bt, tl_rec)
        x1, h2 = _out_proj(xp, olat.reshape(rp, -1), ohg.reshape(rp, -1), onsa.reshape(rp, -1), os5.reshape(rp, -1),
                           cst["wuv_bd"], cst["wout"], g1, sc2, sh2, row1(g_ffn[l]), lp, tm_row)
        xp, cg, cv = _ffn_seq(h2, x1, g2, cst["wg"], cst["wv"], cwg, cwv, cbg, cbv, cst["wd"], lp, tm_row, ck)
        wlen = min(NSA_WINDOW, lp)
        leaves_p.append(dict(
            mla=mla_rows.reshape(bp, lp, MLA_ROW), nsa=nsa_rows.reshape(bp, lp, 4, NSA_D),
            win=win.reshape(bp, lp, 2, NSA_D)[:, lp - wlen:], hgrn=_hgrn_state_from_bd(st),
            s5_re=s5r.reshape(bp, S5_G, S5_N), s5_im=s5i.reshape(bp, S5_G, S5_N),
            conv=jnp.concatenate([cg, cv], axis=-1)))

        m6 = jnp.repeat(mod[l, bp:].reshape(nb, 6, d), ls, axis=0)
        sh1, sc1, g1, sh2, sc2, g2 = (m6[:, i] for i in range(6))
        cq, mla_rows, hgc, qn, nsa_rows, win, gates, s5u = _proj_in(
            xs, sc1, sh1, row1(g_mix[l]), cst["w_perm"], row1(mla_g_kv[l]), cos_s, sin_s, None, rs)
        qlat, qrope = _mla_q(cq, row1(mla_g_q[l]), cst["wuq_perm"], cst["wuk_bd"], tile_h(cos_s), tile_h(sin_s),
                             None, rs)
        seq = lambda a: _pad_t(a.reshape(nb, ls, -1))
        olat = _mla_decode(page_table, seq(qlat), seq(qrope), seq(mla_rows), cache_mla_t, l, npg, nbt, ls)[:, :ls]
        ohg, st = _hgrn(seq(hgc), hg_lb, gout_row, ones_bd, _hgrn_state_to_bd(state_hgrn[l]), bt, TPAD, TPAD, ls, l)
        ohg = ohg[:, :ls]
        o_c, sel = _nsa_cmp_decode(page_table, seq(qn), pool, pair_s, cache_nsa_t, l, npg_cmp, nbt_cmp, past_len)
        sel_steps = sel.reshape(nb, TPAD, n_pages // npg, NSA_X_BLOCKS).transpose(0, 2, 1, 3)
        onsa = _nsa_sel_decode(page_table, seq(qn), sel_steps, kext_s, o_c, seq(nsa_rows), win_t[l], seq(win),
                               seq(gates), cache_nsa_t, l, npg, nbt, past_len, ls)[:, :ls]
        os5, s5r, s5i = _s5(s5u.reshape(nb, ls, -1), perm_s, perm_s.T, *s5_args,
                            state_s5_re[l].reshape(nb, S5_STATE), state_s5_im[l].reshape(nb, S5_STATE), bt, ls)
        x1, h2 = _out_proj(xs, olat.reshape(rs, -1), ohg.reshape(rs, -1), onsa.reshape(rs, -1), os5.reshape(rs, -1),
                           cst["wuv_bd"], cst["wout"], g1, sc2, sh2, row1(g_ffn[l]), None, rs)
        tmaj = lambda a: a.reshape(nb, ls, -1).transpose(1, 0, 2).reshape(rs, -1)
        conv_in = state_ffn_conv[l].transpose(1, 0, 2).reshape((CONV_W - 1) * nb, 2 * dff)
        xs_tm, sg, sv = _ffn_step(tmaj(h2), conv_in[:, :dff], conv_in[:, dff:], tmaj(x1), tmaj(g2), cst["wg"],
                                  cst["wv"], cwg, cwv, cbg, cbv, cst["wd"], nb, ck)
        xs = xs_tm.reshape(ls, nb, d).transpose(1, 0, 2).reshape(rs, d)
        conv_out = jnp.concatenate([sg, sv], axis=-1).reshape(CONV_W - 1, nb, 2 * dff).transpose(1, 0, 2)
        win_state = jnp.concatenate([cache_nsa_win[l], win.reshape(nb, ls, 2, NSA_D)], axis=1)[:, ls:]
        leaves_s.append(dict(
            mla=mla_rows.reshape(nb, ls, MLA_ROW), nsa=nsa_rows.reshape(nb, ls, 4, NSA_D), win=win_state,
            hgrn=_hgrn_state_from_bd(st), s5_re=s5r.reshape(nb, S5_G, S5_N), s5_im=s5i.reshape(nb, S5_G, S5_N),
            conv=conv_out))

    y_prompt = _final_norm(xp, row1(g_final), tm_row).reshape(bp, lp, d)
    y_sample = _final_norm(xs, row1(g_final), rs).reshape(nb, ls, d)
    stk = lambda lst, name: jnp.stack([dd[name] for dd in lst], axis=0)
    out = [y_prompt, y_sample]
    for name in ("mla", "nsa", "win", "hgrn", "s5_re", "s5_im", "conv"):
        out += [stk(leaves_p, name), stk(leaves_s, name)]
    return tuple(out)
```

```python
import functools

import jax
import jax.numpy as jnp
from jax import lax
from jax.experimental import pallas as pl
from jax.experimental.pallas import tpu as pltpu

F32 = jnp.float32
BF16 = jnp.bfloat16

MLA_H, MLA_NOPE, MLA_ROPE, MLA_V = 4, 64, 32, 64
MLA_QLORA, MLA_KVLORA = 384, 128
ROPE_BASE = 10000.0
HG_H, HG_DK, HG_DV = 4, 64, 64
NSA_H, NSA_D, NSA_CMP, NSA_SEL, NSA_TOPN, NSA_WINDOW = 4, 64, 32, 64, 16, 512
S5_CH, S5_G, S5_N = 16, 16, 64
CONV_W = 3
EPS = 1e-6
NEG = -1e30
TINY = 1e-30
BIAS_BIG = 2.0 ** 100

HG_COLS = 2 * HG_H * HG_DK + 2 * HG_H * HG_DV
S5_COLS = S5_G * S5_CH
MLA_ROW = MLA_KVLORA + MLA_ROPE
S5_STATE = S5_G * S5_N

V7X_LANES = 128
V7X_SUBLANES = 8
V7X_VMEM_LIMIT_BYTES = 56 * 1024 * 1024

PC_CQ = 0
PC_CKV = 384
PC_MISC = 512
PC_GATE_LANE = 64
PC_HG = 640
PC_QN = 1664
PC_NSA = 1920
PC_WIN = 2176
PC_S5 = 2304
PC_TOTAL = 2560

NSA_XW = 64
NSA_X_BLOCKS = 32
NSA_X_HI = 32
NSA_X_LO = 33
NSA_X_ONE = 34


def _cparams(*semantics):
    return pltpu.CompilerParams(dimension_semantics=tuple(semantics), vmem_limit_bytes=V7X_VMEM_LIMIT_BYTES)


def _dot(a, b):
    return jnp.dot(a, b, preferred_element_type=F32)


def _dot_nt(a, b):
    return lax.dot_general(a, b, (((1,), (1,)), ((), ())), preferred_element_type=F32)


def _dot_tn(a, b):
    return lax.dot_general(a, b, (((0,), (0,)), ((), ())), preferred_element_type=F32)


def _split(a):
    hi = a.astype(BF16)
    return hi, (a - hi.astype(F32)).astype(BF16)


def _dot_split(a, b_bf16):
    hi, lo = _split(a)
    return _dot(hi, b_bf16) + _dot(lo, b_bf16)


def _silu(x):
    return x * jax.nn.sigmoid(x)


def _mod_kernel(c_ref, w_ref, b_ref, o_ref):
    s = _silu(c_ref[...]).astype(BF16)
    o_ref[...] = _dot(s, w_ref[...].astype(BF16)) + b_ref[...]


def _modulation(c_all, w_ada, b_ada):
    depth, d, n = w_ada.shape
    bc = c_all.shape[0]
    tn = 1536
    return pl.pallas_call(
        _mod_kernel,
        out_shape=jax.ShapeDtypeStruct((depth, bc, n), F32),
        grid=(depth, n // tn),
        in_specs=[pl.BlockSpec((bc, d), lambda l, j: (0, 0)),
                  pl.BlockSpec((None, d, tn), lambda l, j: (l, 0, j)),
                  pl.BlockSpec((None, 1, tn), lambda l, j: (l, 0, j))],
        out_specs=pl.BlockSpec((None, bc, tn), lambda l, j: (l, 0, j)),
        compiler_params=_cparams("parallel", "parallel"),
        name="adaln_modulation",
    )(c_all, w_ada, b_ada.reshape(depth, 1, n))


def _proj_in_kernel(x_ref, sc_ref, sh_ref, g_ref, w_ref, gkv_ref, cos_ref, sin_ref,
                    cq_ref, mla_ref, hg_ref, qn_ref, nsa_ref, win_ref, gate_ref, s5_ref):
    x = x_ref[...]
    var = jnp.mean(x * x, axis=-1, keepdims=True)
    h = (x * lax.rsqrt(var + EPS) * g_ref[...]) * (1.0 + sc_ref[...]) + sh_ref[...]
    cols = _dot(h.astype(BF16), w_ref[...])
    cq_ref[...] = cols[:, PC_CQ:PC_CQ + MLA_QLORA]
    ckv = cols[:, PC_CKV:PC_CKV + MLA_KVLORA]
    kvar = jnp.mean(ckv * ckv, axis=-1, keepdims=True)
    mla_ref[:, 0:MLA_KVLORA] = ckv * lax.rsqrt(kvar + EPS) * gkv_ref[...]
    kr = cols[:, PC_MISC:PC_MISC + MLA_ROPE]
    kr_swapped = cols[:, PC_MISC + MLA_ROPE:PC_MISC + 2 * MLA_ROPE]
    mla_ref[:, MLA_KVLORA:MLA_ROW] = kr * cos_ref[...] + kr_swapped * sin_ref[...]
    gate_ref[...] = jax.nn.sigmoid(cols[:, PC_MISC:PC_MISC + V7X_LANES])
    hg_ref[...] = cols[:, PC_HG:PC_HG + HG_COLS]
    qn_ref[...] = cols[:, PC_QN:PC_QN + NSA_H * NSA_D] * (NSA_D ** -0.5)
    nsa_ref[...] = cols[:, PC_NSA:PC_NSA + 4 * NSA_D]
    win_ref[...] = cols[:, PC_WIN:PC_WIN + 2 * NSA_D]
    s5_ref[...] = cols[:, PC_S5:PC_S5 + S5_COLS]


def _proj_in(x2d, scale, shift, g_mix, w_perm, g_kv, cos_t, sin_t, rows_per_seq, tm):
    r, d = x2d.shape
    if rows_per_seq is not None:
        tps = rows_per_seq // tm
        mod_spec = pl.BlockSpec((None, 1, d), lambda i: (i // tps, 0, 0))
        rope_spec = pl.BlockSpec((tm, MLA_ROPE), lambda i: (i % tps, 0))
    else:
        mod_spec = pl.BlockSpec((tm, d), lambda i: (i, 0))
        rope_spec = pl.BlockSpec((tm, MLA_ROPE), lambda i: (i, 0))
    row = lambda n: pl.BlockSpec((tm, n), lambda i: (i, 0))
    const = lambda a, b: pl.BlockSpec((a, b), lambda i: (0, 0))
    widths = (MLA_QLORA, MLA_ROW, HG_COLS, NSA_H * NSA_D, 4 * NSA_D, 2 * NSA_D, V7X_LANES, S5_COLS)
    return pl.pallas_call(
        _proj_in_kernel,
        out_shape=[jax.ShapeDtypeStruct((r, n), F32) for n in widths],
        grid=(r // tm,),
        in_specs=[row(d), mod_spec, mod_spec, const(1, d), const(d, PC_TOTAL), const(1, MLA_KVLORA),
                  rope_spec, rope_spec],
        out_specs=[row(n) for n in widths],
        compiler_params=_cparams("parallel"),
        name="proj_in",
    )(x2d, scale, shift, g_mix, w_perm, g_kv, cos_t, sin_t)


def _mla_q_kernel(cq_ref, g_ref, wuq_ref, wuk_ref, cos_ref, sin_ref, qlat_ref, qrope_ref):
    cq = cq_ref[...]
    var = jnp.mean(cq * cq, axis=-1, keepdims=True)
    c = cq * lax.rsqrt(var + EPS) * g_ref[...]
    q = _dot(c.astype(BF16), wuq_ref[...])
    scale = (MLA_NOPE + MLA_ROPE) ** -0.5
    n_nope = MLA_H * MLA_NOPE
    n_rope = MLA_H * MLA_ROPE
    qlat_ref[...] = _dot(q[:, :n_nope].astype(BF16), wuk_ref[...]) * scale
    qrope_ref[...] = (q[:, n_nope:n_nope + n_rope] * cos_ref[...]
                      + q[:, n_nope + n_rope:] * sin_ref[...]) * scale


def _mla_q(cq2d, g_q, wuq_perm, wuk_bd, cos_t, sin_t, rows_per_seq, tm):
    r = cq2d.shape[0]
    n_rope = MLA_H * MLA_ROPE
    if rows_per_seq is not None:
        tps = rows_per_seq // tm
        rope_spec = pl.BlockSpec((tm, n_rope), lambda i: (i % tps, 0))
    else:
        rope_spec = pl.BlockSpec((tm, n_rope), lambda i: (i, 0))
    row = lambda n: pl.BlockSpec((tm, n), lambda i: (i, 0))
    const = lambda a, b: pl.BlockSpec((a, b), lambda i: (0, 0))
    n_lat = MLA_H * MLA_KVLORA
    return pl.pallas_call(
        _mla_q_kernel,
        out_shape=[jax.ShapeDtypeStruct((r, n_lat), F32), jax.ShapeDtypeStruct((r, n_rope), F32)],
        grid=(r // tm,),
        in_specs=[row(MLA_QLORA), const(1, MLA_QLORA), const(MLA_QLORA, wuq_perm.shape[1]),
                  const(MLA_H * MLA_NOPE, n_lat), rope_spec, rope_spec],
        out_specs=[row(n_lat), row(n_rope)],
        compiler_params=_cparams("parallel"),
        name="mla_q_prep",
    )(cq2d, g_q, wuq_perm, wuk_bd, cos_t, sin_t)


def _softmax_step(s, mask, m, l, acc, pv):
    if mask is not None:
        s = jnp.where(mask, s, NEG)
    m_new = jnp.maximum(m, jnp.max(s, axis=-1, keepdims=True))
    alpha = jnp.exp(m - m_new)
    p = jnp.exp(s - m_new)
    if mask is not None:
        p = jnp.where(mask, p, 0.0)
    l_new = alpha * l + jnp.sum(p, axis=-1, keepdims=True)
    acc_new = alpha * acc + pv(p.astype(BF16))
    return m_new, l_new, acc_new


def _softmax_step_sum(s, mask, m, acc, pv):
    if mask is not None:
        s = jnp.where(mask, s, NEG)
    m_new = jnp.maximum(m, jnp.max(s, axis=-1, keepdims=True))
    p = jnp.exp(s - m_new)
    if mask is not None:
        p = jnp.where(mask, p, 0.0)
    return m_new, jnp.exp(m - m_new) * acc + pv(p.astype(BF16))


def _softmax_init(rows, dv):
    return jnp.full((rows, 1), NEG, F32), jnp.zeros((rows, 1), F32), jnp.zeros((rows, dv), F32)


def _stack_mla_q(qlat, qrope):
    parts = []
    for h in range(MLA_H):
        parts.append(jnp.concatenate([qlat[:, h * MLA_KVLORA:(h + 1) * MLA_KVLORA],
                                      qrope[:, h * MLA_ROPE:(h + 1) * MLA_ROPE]], axis=-1))
    return jnp.concatenate(parts, axis=0).astype(BF16)


def _mla_prompt_kernel(qlat_ref, qrope_ref, keys_ref, o_ref, *, tq, tk):
    i = pl.program_id(1)
    q = _stack_mla_q(qlat_ref[...], qrope_ref[...])
    rows = MLA_H * tq

    def chunk(c, carry, mask):
        start = pl.multiple_of(c * tk, tk)
        kblk = keys_ref[pl.ds(start, tk), :].astype(BF16)
        return _softmax_step(_dot_nt(q, kblk), mask, *carry, lambda p: _dot(p, kblk[:, :MLA_KVLORA]))

    diag = (i * tq) // tk
    carry = lax.fori_loop(0, diag, lambda c, carry: chunk(c, carry, None), _softmax_init(rows, MLA_KVLORA))
    qofs = i * tq - diag * tk + lax.rem(lax.broadcasted_iota(jnp.int32, (rows, tk), 0), tq)
    m, l, acc = chunk(diag, carry, lax.broadcasted_iota(jnp.int32, (rows, tk), 1) <= qofs)
    o = acc / jnp.maximum(l, TINY)
    o_ref[...] = jnp.concatenate([o[h * tq:(h + 1) * tq] for h in range(MLA_H)], axis=-1)


def _mla_prompt(qlat, qrope, keys, tq, tk):
    b, l, _ = qlat.shape
    n_lat = MLA_H * MLA_KVLORA
    assert tk % tq == 0
    return pl.pallas_call(
        functools.partial(_mla_prompt_kernel, tq=tq, tk=tk),
        out_shape=jax.ShapeDtypeStruct((b, l, n_lat), F32),
        grid=(b, l // tq),
        in_specs=[pl.BlockSpec((None, tq, n_lat), lambda bi, i: (bi, i, 0)),
                  pl.BlockSpec((None, tq, MLA_H * MLA_ROPE), lambda bi, i: (bi, i, 0)),
                  pl.BlockSpec((None, l, MLA_ROW), lambda bi, i: (bi, 0, 0))],
        out_specs=pl.BlockSpec((None, tq, n_lat), lambda bi, i: (bi, i, 0)),
        compiler_params=_cparams("parallel", "parallel"),
        name="mla_prompt_attention",
    )(qlat, qrope, keys)


def _hgrn_kernel(hg_ref, lb_ref, gout_ref, ones_ref, s0_ref, o_ref, st_ref,
                 st_scr, qe_scr, ke_scr, v_scr, dl_scr, o_scr, *, bt, tl, c, l_valid, layer):
    j = pl.program_id(1)
    nch = tl // c
    rows = bt * tl
    g = rows // c
    width = HG_H * HG_DK

    @pl.when(j == 0)
    def _():
        st_scr[...] = s0_ref[...]

    x = hg_ref[...].reshape(rows, HG_COLS)
    hq = x[:, 0:width]
    z = x[:, width:2 * width]
    vi = x[:, 2 * width:3 * width]
    hgate = x[:, 3 * width:4 * width]
    raw = lb_ref[...]
    e = jnp.exp(raw - jnp.max(raw, axis=0, keepdims=True))
    lbs = e / jnp.sum(e, axis=0, keepdims=True)
    lb = jnp.zeros((1, width), F32)
    for dd in range(1, layer + 1):
        lb = lb + lbs[dd:dd + 1, :]
    t_in = lax.rem(lax.broadcasted_iota(jnp.int32, (rows, width), 0), c)
    valid = (j * tl + lax.rem(lax.broadcasted_iota(jnp.int32, (rows, width), 0), tl)) < l_valid
    logf = jnp.where(valid, jnp.log(lb + (1.0 - lb) * jax.nn.sigmoid(z)), 0.0)
    kg = jnp.where(valid, (1.0 - lb) * jax.nn.sigmoid(-z), 0.0)
    qg = _silu(hq)
    b = logf
    k = 1
    while k < c:
        b = b + jnp.where(t_in >= k, pltpu.roll(b, k, 0), 0.0)
        k *= 2
    b3 = b.reshape(g, c, width)
    q3 = qg.reshape(g, c, width)
    k3 = kg.reshape(g, c, width)
    v3 = vi.reshape(g, c, width)
    ones_bd = ones_ref[...]
    tt = lax.broadcasted_iota(jnp.int32, (g, c, width), 1)
    o = jnp.zeros((g, c, width), F32)
    for s in range(c):
        bs = b3[:, s:s + 1, :]
        w = q3 * k3[:, s:s + 1, :] * jnp.exp(jnp.where(tt >= s, b3 - bs, NEG))
        r = _dot(w.reshape(rows, width).astype(BF16), ones_bd)
        o = o + r.reshape(g, c, width) * v3[:, s:s + 1, :]
    bl = b3[:, c - 1:c, :]
    o_scr[...] = o.reshape(bt, nch, c, width)
    qe_scr[...] = (q3 * jnp.exp(b3)).reshape(bt, nch, c, width).astype(BF16)
    ke_scr[...] = (k3 * jnp.exp(bl - b3)).reshape(bt, nch, c, width).astype(BF16)
    v_scr[...] = v3.reshape(bt, nch, c, width).astype(BF16)
    dl_scr[...] = jnp.exp(bl).reshape(bt, nch, 1, width)
    bd_mask = ones_bd.astype(F32)

    def chunk(n, carry):
        for bi in range(bt):
            st = st_scr[bi]
            o_scr[bi, n] = o_scr[bi, n] + _dot_nt(qe_scr[bi, n], st.astype(BF16))
            upd = _dot_tn(v_scr[bi, n], ke_scr[bi, n])
            st_scr[bi] = st * dl_scr[bi, n] + upd * bd_mask
        return carry

    lax.fori_loop(0, nch, chunk, 0)
    oo = o_scr[...].reshape(rows, width)
    ms = _dot_split(oo * oo, ones_bd) * (1.0 / HG_DV)
    on = oo * lax.rsqrt(ms + EPS) * gout_ref[...]
    o_ref[...] = (on * _silu(hgate)).reshape(bt, tl, width)

    @pl.when(j == pl.num_programs(1) - 1)
    def _():
        st_ref[...] = st_scr[...]


def _hgrn(hgc, hg_lb, gout_row, ones_bd, st0, bt, tl, c, l_valid, layer):
    b, l, _ = hgc.shape
    width = HG_H * HG_DK
    nch = tl // c
    return pl.pallas_call(
        functools.partial(_hgrn_kernel, bt=bt, tl=tl, c=c, l_valid=l_valid, layer=layer),
        out_shape=[jax.ShapeDtypeStruct((b, l, width), F32), jax.ShapeDtypeStruct((b, width, width), F32)],
        grid=(b // bt, l // tl),
        in_specs=[pl.BlockSpec((bt, tl, HG_COLS), lambda i, j: (i, j, 0)),
                  pl.BlockSpec(hg_lb.shape, lambda i, j: (0, 0)),
                  pl.BlockSpec((1, width), lambda i, j: (0, 0)),
                  pl.BlockSpec((width, width), lambda i, j: (0, 0)),
                  pl.BlockSpec((bt, width, width), lambda i, j: (i, 0, 0))],
        out_specs=[pl.BlockSpec((bt, tl, width), lambda i, j: (i, j, 0)),
                   pl.BlockSpec((bt, width, width), lambda i, j: (i, 0, 0))],
        scratch_shapes=[pltpu.VMEM((bt, width, width), F32),
                        pltpu.VMEM((bt, nch, c, width), BF16),
                        pltpu.VMEM((bt, nch, c, width), BF16),
                        pltpu.VMEM((bt, nch, c, width), BF16),
                        pltpu.VMEM((bt, nch, 1, width), F32),
                        pltpu.VMEM((bt, nch, c, width), F32)],
        compiler_params=_cparams("parallel", "arbitrary"),
        name="hgrn2_chunked",
    )(hgc, hg_lb, gout_row, ones_bd, st0)


def _s5_kernel(u_ref, perm_ref, permt_ref, br_ref, bi_ref, are_ref, aim_ref, ldt_ref, cre_ref, cim_ref, d_ref,
               wglu_ref, bglu_ref, x0r_ref, x0i_ref, y_ref, xr_out, xi_out,
               p_scr, q_scr, xr_scr, xi_scr, cr_scr, ci_scr, *, bt, tl):
    j = pl.program_id(1)
    n = bt * tl

    @pl.when(j == 0)
    def _():
        cr_scr[...] = x0r_ref[...]
        ci_scr[...] = x0i_ref[...]

    u = u_ref[...].reshape(n, S5_COLS)
    u_tm = _dot(perm_ref[...], u.astype(BF16)).astype(BF16)
    p_scr[...] = _dot(u_tm, br_ref[...])
    q_scr[...] = _dot(u_tm, bi_ref[...])
    ar, ai = are_ref[...], aim_ref[...]
    dt = jnp.exp(ldt_ref[...])
    mag = jnp.exp(dt * ar)
    abr1, abi1 = mag * jnp.cos(dt * ai), mag * jnp.sin(dt * ai)
    den = ar * ar + ai * ai
    fr1 = ((abr1 - 1.0) * ar + abi1 * ai) / den
    fi1 = (abi1 * ar - (abr1 - 1.0) * ai) / den
    shape = (bt, S5_STATE)
    abr, abi = jnp.broadcast_to(abr1, shape), jnp.broadcast_to(abi1, shape)
    fr, fi = jnp.broadcast_to(fr1, shape), jnp.broadcast_to(fi1, shape)

    def step(t, carry):
        xr, xi = carry
        off = pl.multiple_of(t * bt, bt)
        p = p_scr[pl.ds(off, bt), :]
        q = q_scr[pl.ds(off, bt), :]
        nxr = abr * xr - abi * xi + (fr * p - fi * q)
        nxi = abr * xi + abi * xr + (fr * q + fi * p)
        xr_scr[pl.ds(off, bt), :] = nxr
        xi_scr[pl.ds(off, bt), :] = nxi
        return nxr, nxi

    xr, xi = lax.fori_loop(0, tl, step, (cr_scr[...], ci_scr[...]))
    cr_scr[...] = xr
    ci_scr[...] = xi
    yc_tm = _dot(xr_scr[...].astype(BF16), cre_ref[...]) - _dot(xi_scr[...].astype(BF16), cim_ref[...])
    hi, lo = _split(yc_tm)
    yc = _dot(permt_ref[...], hi) + _dot(permt_ref[...], lo)
    y = jax.nn.gelu(yc + d_ref[...] * u)
    y = y * jax.nn.sigmoid(_dot(y.astype(BF16), wglu_ref[...]) + bglu_ref[...])
    y_ref[...] = y.reshape(bt, tl, S5_COLS)

    @pl.when(j == pl.num_programs(1) - 1)
    def _():
        xr_out[...] = xr
        xi_out[...] = xi


def _s5(u, perm, permt, br_bd, bi_bd, are_row, aim_row, ldt_row, cre_bd, cim_bd, d_row, wglu, bglu_row, x0r, x0i, bt, tl):
    b, l, _ = u.shape
    n = bt * tl
    const = lambda a, bb: pl.BlockSpec((a, bb), lambda i, j: (0, 0))
    state = pl.BlockSpec((bt, S5_STATE), lambda i, j: (i, 0))
    return pl.pallas_call(
        functools.partial(_s5_kernel, bt=bt, tl=tl),
        out_shape=[jax.ShapeDtypeStruct((b, l, S5_COLS), F32),
                   jax.ShapeDtypeStruct((b, S5_STATE), F32), jax.ShapeDtypeStruct((b, S5_STATE), F32)],
        grid=(b // bt, l // tl),
        in_specs=[pl.BlockSpec((bt, tl, S5_COLS), lambda i, j: (i, j, 0)),
                  const(n, n), const(n, n), const(S5_COLS, S5_STATE), const(S5_COLS, S5_STATE),
                  const(1, S5_STATE), const(1, S5_STATE), const(1, S5_STATE),
                  const(S5_STATE, S5_COLS), const(S5_STATE, S5_COLS), const(1, S5_COLS),
                  const(S5_COLS, S5_COLS), const(1, S5_COLS), state, state],
        out_specs=[pl.BlockSpec((bt, tl, S5_COLS), lambda i, j: (i, j, 0)), state, state],
        scratch_shapes=[pltpu.VMEM((n, S5_STATE), F32), pltpu.VMEM((n, S5_STATE), F32),
                        pltpu.VMEM((n, S5_STATE), F32), pltpu.VMEM((n, S5_STATE), F32),
                        pltpu.VMEM((bt, S5_STATE), F32), pltpu.VMEM((bt, S5_STATE), F32)],
        compiler_params=_cparams("parallel", "arbitrary"),
        name="s5_scan",
    )(u, perm, permt, br_bd, bi_bd, are_row, aim_row, ldt_row, cre_bd, cim_bd, d_row, wglu, bglu_row, x0r, x0i)


def _alibi_rows(rows, t):
    head = lax.broadcasted_iota(jnp.int32, (rows, 1), 0) // t
    slope = jnp.zeros((rows, 1), F32)
    for h in range(NSA_H):
        slope = jnp.where(head == h, 2.0 ** (-8.0 * (h + 1) / NSA_H), slope)
    return slope


def _stack_heads(x, width):
    return jnp.concatenate([x[:, h * width:(h + 1) * width] for h in range(NSA_H)], axis=0)


def _nsa_query_ext(q, slope, sel_bias, tile_bias):
    rows = q.shape[0]
    lane = lax.broadcasted_iota(jnp.int32, (rows, NSA_XW - NSA_X_BLOCKS), 1) + NSA_X_BLOCKS
    tail = jnp.where(lane == NSA_X_HI, float(V7X_LANES) * slope, jnp.where(lane == NSA_X_LO, slope, 0.0))
    if tile_bias is not None:
        tail = jnp.where(lane == NSA_X_ONE, tile_bias, tail)
    if sel_bias is None:
        sel_bias = jnp.zeros((rows, NSA_X_BLOCKS), F32)
    return jnp.concatenate([q, sel_bias, tail], axis=-1).astype(BF16)


def _nsa_compressed(s, mask, pv, t):
    s = jnp.where(mask, s, NEG)
    m = jnp.max(s, axis=-1, keepdims=True)
    e = jnp.where(mask, jnp.exp(s - m), 0.0)
    p = e / jnp.maximum(jnp.sum(e, axis=-1, keepdims=True), TINY)
    psum = p[0:t]
    for h in range(1, NSA_H):
        psum = psum + p[h * t:(h + 1) * t]
    return pv(p.astype(BF16)), psum


def _nsa_rank_select(imp, blk, qpos, n_top, axis):
    imp = jnp.where(blk == qpos // NSA_SEL, jnp.inf, jnp.where(blk * NSA_SEL <= qpos, imp, -jnp.inf))
    rank = jnp.zeros(imp.shape, jnp.int32)
    for c in range(imp.shape[axis]):
        other = imp[c:c + 1, :] if axis == 0 else imp[:, c:c + 1]
        rank = rank + jnp.where(other > imp, 1, jnp.where(other == imp, jnp.where(blk > c, 1, 0), 0))
    return jnp.where(rank < n_top, 1.0, 0.0)


def _nsa_combine(gates, o_c, o_s, o_w, t):
    outs = []
    for h in range(NSA_H):
        lane = PC_GATE_LANE + 3 * h
        rs = slice(h * t, (h + 1) * t)
        outs.append(gates[:, lane:lane + 1] * o_c[rs] + gates[:, lane + 1:lane + 2] * o_s[rs]
                    + gates[:, lane + 2:lane + 3] * o_w[rs])
    return jnp.concatenate(outs, axis=-1)


def _nsa_prompt_kernel(q_ref, rows_ref, win_ref, gate_ref, pairt_ref, kext_ref, eye_ref, o_ref,
                       cm_scr, ks_scr, kw_scr, vs_scr, vw_scr, *, tq, tk, l):
    i = pl.program_id(1)
    ncmp = l // NSA_CMP
    nsel = l // NSA_SEL
    d = NSA_D

    @pl.when(i == 0)
    def _():
        kv = rows_ref[:, 0:2 * d]
        cm_scr[...] = jnp.sum(kv.reshape(ncmp, NSA_CMP, 2 * d), axis=1) * (1.0 / NSA_CMP)
        kext = kext_ref[...]
        ks_scr[...] = jnp.concatenate([rows_ref[:, 2 * d:3 * d].astype(BF16), kext], axis=-1)
        kw_scr[...] = jnp.concatenate([win_ref[:, 0:d].astype(BF16), kext], axis=-1)
        vs_scr[...] = jnp.concatenate([rows_ref[:, 3 * d:4 * d].astype(BF16), kext], axis=-1)
        vw_scr[...] = jnp.concatenate([win_ref[:, d:2 * d].astype(BF16), kext], axis=-1)

    rows = NSA_H * tq
    q = _stack_heads(q_ref[...], d)
    slope = _alibi_rows(rows, tq)
    qofs = lax.rem(lax.broadcasted_iota(jnp.int32, (rows, 1), 0), tq)
    qpos = i * tq + qofs

    cm = cm_scr[...]
    cend = lax.broadcasted_iota(jnp.int32, (rows, ncmp), 1) * NSA_CMP + (NSA_CMP - 1)
    dc = qpos - cend
    s_c = _dot_nt(q.astype(BF16), cm[:, 0:d].astype(BF16)) - slope * dc.astype(F32)
    vcm = cm[:, d:2 * d].astype(BF16)
    o_c, psum = _nsa_compressed(s_c, dc >= 0, lambda p: _dot(p, vcm), tq)

    hi, lo = _split(psum)
    imp_t = _dot_nt(pairt_ref[...], hi) + _dot_nt(pairt_ref[...], lo)
    blk_t = lax.broadcasted_iota(jnp.int32, (nsel, tq), 0)
    qpos_t = i * tq + lax.broadcasted_iota(jnp.int32, (nsel, tq), 1)
    sel_t = _nsa_rank_select(imp_t, blk_t, qpos_t, min(NSA_TOPN, nsel), 0)
    sel = _dot_tn(sel_t.astype(BF16), eye_ref[...])
    sel_bias = jnp.concatenate([(sel - 1.0) * BIAS_BIG] * NSA_H, axis=0)

    q_sel = _nsa_query_ext(q, slope, sel_bias, None)
    acc0 = (jnp.full((rows, 1), NEG, F32), jnp.zeros((rows, d + NSA_XW), F32))

    def sel_chunk(c, carry, mask):
        start = pl.multiple_of(c * tk, tk)
        s = _dot_nt(q_sel, ks_scr[pl.ds(start, tk), :])
        vblk = vs_scr[pl.ds(start, tk), :]
        return _softmax_step_sum(s, mask, *carry, lambda p: _dot(p, vblk))

    diag = (i * tq) // tk
    carry = lax.fori_loop(0, diag, lambda c, carry: sel_chunk(c, carry, None), acc0)
    kofs = lax.broadcasted_iota(jnp.int32, (rows, tk), 1)
    m, acc = sel_chunk(diag, carry, kofs <= (i * tq - diag * tk) + qofs)
    o_s = acc[:, 0:d] / jnp.maximum(acc[:, d + NSA_X_ONE:d + NSA_X_ONE + 1], TINY)

    q_win = _nsa_query_ext(q, slope, None, None)
    wb = NSA_WINDOW + tq
    wstart = jnp.maximum(i * tq - NSA_WINDOW, 0)
    wofs = lax.broadcasted_iota(jnp.int32, (rows, wb), 1) + (wstart - i * tq)
    s = _dot_nt(q_win, kw_scr[pl.ds(pl.multiple_of(wstart, tq), wb), :])
    vblk = vw_scr[pl.ds(pl.multiple_of(wstart, tq), wb), :]
    m, acc = _softmax_step_sum(s, (wofs <= qofs) & (wofs > qofs - NSA_WINDOW), *acc0, lambda p: _dot(p, vblk))
    o_w = acc[:, 0:d] / jnp.maximum(acc[:, d + NSA_X_ONE:d + NSA_X_ONE + 1], TINY)
    o_ref[...] = _nsa_combine(gate_ref[...], o_c, o_s, o_w, tq)


def _nsa_prompt(qn, rows, win, gates, pairt, kext, eye, tq, tk):
    b, l, _ = qn.shape
    d = NSA_D
    assert tk % tq == 0 and NSA_WINDOW % tq == 0 and l // NSA_SEL <= NSA_X_BLOCKS and l >= NSA_WINDOW + tq
    seq = lambda n: pl.BlockSpec((None, l, n), lambda bi, i: (bi, 0, 0))
    blk = lambda n: pl.BlockSpec((None, tq, n), lambda bi, i: (bi, i, 0))
    const = lambda a: pl.BlockSpec(a.shape, lambda bi, i: (0, 0))
    return pl.pallas_call(
        functools.partial(_nsa_prompt_kernel, tq=tq, tk=tk, l=l),
        out_shape=jax.ShapeDtypeStruct((b, l, NSA_H * d), F32),
        grid=(b, l // tq),
        in_specs=[blk(NSA_H * d), seq(4 * d), seq(2 * d), blk(V7X_LANES), const(pairt), const(kext), const(eye)],
        out_specs=blk(NSA_H * d),
        scratch_shapes=[pltpu.VMEM((l // NSA_CMP, 2 * d), F32),
                        pltpu.VMEM((l, d + NSA_XW), BF16), pltpu.VMEM((l, d + NSA_XW), BF16),
                        pltpu.VMEM((l, d + NSA_XW), BF16), pltpu.VMEM((l, d + NSA_XW), BF16)],
        compiler_params=_cparams("parallel", "arbitrary"),
        name="nsa_prompt_attention",
    )(qn, rows, win, gates, pairt, kext, eye)


TPAD = V7X_SUBLANES


class _PagePipe:
    def __init__(self, cache_ref, pt_ref, buf, sem, layer, row0, feat, npg, nbt, n_steps, n_groups):
        self.cache_ref, self.pt_ref, self.buf, self.sem = cache_ref, pt_ref, buf, sem
        self.layer, self.row0, self.feat = layer, row0, feat
        self.npg, self.nbt, self.n_steps, self.n_groups = npg, nbt, n_steps, n_groups

    def _copies(self, g, step, slot):
        out = []
        for bi in range(self.nbt):
            for jj in range(self.npg):
                phys = self.pt_ref[g * self.nbt + bi, step * self.npg + jj]
                src = self.cache_ref.at[self.layer, phys, pl.ds(self.row0, self.feat), :]
                out.append(pltpu.make_async_copy(src, self.buf.at[slot, bi * self.npg + jj], self.sem.at[slot]))
        return out

    def start(self, g, step, slot):
        for c in self._copies(g, step, slot):
            c.start()

    def wait(self, g, step, slot):
        for c in self._copies(g, step, slot):
            c.wait()

    def run(self, init, body, finish):
        n_steps, n_groups = self.n_steps, self.n_groups
        self.start(0, 0, 0)

        def group(g, _):
            def step(k, carry):
                slot = lax.rem(g * n_steps + k, 2)
                self.start(g, k + 1, 1 - slot)
                self.wait(g, k, slot)
                return body(g, k, slot, carry)

            carry = lax.fori_loop(0, n_steps - 1, step, init(g))
            slot = lax.rem(g * n_steps + n_steps - 1, 2)
            self.start(jnp.minimum(g + 1, n_groups - 1), 0, 1 - slot)
            self.wait(g, n_steps - 1, slot)
            finish(g, body(g, n_steps - 1, slot, carry))
            return 0

        lax.fori_loop(0, n_groups, group, 0)
        self.wait(n_groups - 1, 0, (n_groups * n_steps) % 2)


def _mla_decode_kernel(pt_ref, qlat_ref, qrope_ref, knew_ref, cache_ref, o_ref, buf, sem, *, npg, nbt, layer, l_new):
    nb, n_pages = pt_ref.shape
    page = buf.shape[-1]
    rows = MLA_H * TPAD
    pipe = _PagePipe(cache_ref, pt_ref, buf, sem, layer, 0, MLA_ROW, npg, nbt, n_pages // npg, nb // nbt)
    tq = lax.rem(lax.broadcasted_iota(jnp.int32, (rows, TPAD), 0), TPAD)
    tk = lax.broadcasted_iota(jnp.int32, (rows, TPAD), 1)
    new_mask = (tk <= tq) & (tk < l_new)

    def queries(g):
        return [_stack_mla_q(qlat_ref[g * nbt + bi], qrope_ref[g * nbt + bi]) for bi in range(nbt)]

    def init(g):
        return tuple(_softmax_init(rows, MLA_KVLORA) for _ in range(nbt))

    def body(g, k, slot, carry):
        qs = queries(g)
        new = []
        for bi in range(nbt):
            pages = [buf[slot, bi * npg + jj].astype(BF16) for jj in range(npg)]
            s = jnp.concatenate([_dot(qs[bi], pg) for pg in pages], axis=-1)

            def pv(p, pages=pages):
                out = _dot_nt(p[:, 0:page], pages[0][0:MLA_KVLORA, :])
                for jj in range(1, npg):
                    out = out + _dot_nt(p[:, jj * page:(jj + 1) * page], pages[jj][0:MLA_KVLORA, :])
                return out

            new.append(_softmax_step(s, None, *carry[bi], pv))
        return tuple(new)

    def finish(g, carry):
        qs = queries(g)
        for bi in range(nbt):
            knew = knew_ref[g * nbt + bi].astype(BF16)
            mm, ll, aa = _softmax_step(_dot_nt(qs[bi], knew), new_mask, *carry[bi],
                                       lambda p, knew=knew: _dot(p, knew[:, :MLA_KVLORA]))
            o = aa / jnp.maximum(ll, TINY)
            o_ref[g * nbt + bi] = jnp.concatenate([o[h * TPAD:(h + 1) * TPAD] for h in range(MLA_H)], axis=-1)

    pipe.run(init, body, finish)


def _paged_call(kernel_fn, name, page_table, vmem_inputs, cache_t, out_shapes, feat, npg, nbt, extra_scratch=()):
    page = cache_t.shape[3]
    whole = lambda a: pl.BlockSpec(a.shape, lambda i, pt, n=len(a.shape): (0,) * n)
    grid_spec = pltpu.PrefetchScalarGridSpec(
        num_scalar_prefetch=1,
        grid=(1,),
        in_specs=[whole(a) for a in vmem_inputs] + [pl.BlockSpec(memory_space=pl.ANY)],
        out_specs=[whole(o) for o in out_shapes],
        scratch_shapes=[pltpu.VMEM((2, nbt * npg, feat, page), F32), pltpu.SemaphoreType.DMA((2,))]
                       + list(extra_scratch))
    return pl.pallas_call(kernel_fn, out_shape=out_shapes, grid_spec=grid_spec,
                          compiler_params=_cparams("arbitrary"), name=name)(page_table, *vmem_inputs, cache_t)


def _mla_decode(page_table, qlat, qrope, knew, cache_t, layer, npg, nbt, l_new):
    nb = page_table.shape[0]
    out = [jax.ShapeDtypeStruct((nb, TPAD, MLA_H * MLA_KVLORA), F32)]
    return _paged_call(functools.partial(_mla_decode_kernel, npg=npg, nbt=nbt, layer=layer, l_new=l_new),
                       "mla_paged_decode", page_table, [qlat, qrope, knew], cache_t, out, MLA_ROW, npg, nbt)[0]


def _nsa_cmp_decode_kernel(pt_ref, q_ref, pool_ref, pair_ref, cache_ref, oc_ref, sel_ref, buf, sem, cm_scr, *,
                           npg, nbt, layer, past_len):
    nb, n_pages = pt_ref.shape
    d = NSA_D
    n_steps = n_pages // npg
    rows = NSA_H * TPAD
    pipe = _PagePipe(cache_ref, pt_ref, buf, sem, layer, 0, 2 * d, npg, nbt, n_steps, nb // nbt)
    slope = _alibi_rows(rows, TPAD)
    qpos = past_len + lax.rem(lax.broadcasted_iota(jnp.int32, (rows, 1), 0), TPAD)
    ncmp = past_len // NSA_CMP
    nsel = past_len // NSA_SEL
    cend = lax.broadcasted_iota(jnp.int32, (rows, ncmp), 1) * NSA_CMP + (NSA_CMP - 1)
    dc = qpos - cend
    blk = lax.broadcasted_iota(jnp.int32, (TPAD, nsel), 1)

    def body(g, k, slot, carry):
        for bi in range(nbt):
            acc = jnp.zeros((2 * d, cm_scr.shape[3]), F32)
            for jj in range(npg):
                hi, lo = _split(buf[slot, bi * npg + jj])
                acc = acc + _dot(hi, pool_ref[jj]) + _dot(lo, pool_ref[jj])
            cm_scr[bi, k] = acc
        return carry

    def finish(g, carry):
        for bi in range(nbt):
            q = _stack_heads(q_ref[g * nbt + bi], d).astype(BF16)
            cm_t = jnp.concatenate([cm_scr[bi, kk] for kk in range(n_steps)], axis=-1)
            s_c = _dot(q, cm_t[0:d].astype(BF16)) - slope * dc.astype(F32)
            vcm_t = cm_t[d:2 * d].astype(BF16)
            o_c, psum = _nsa_compressed(s_c, dc >= 0, lambda p, vcm_t=vcm_t: _dot_nt(p, vcm_t), TPAD)
            oc_ref[g * nbt + bi] = o_c
            sel_ref[g * nbt + bi] = _nsa_rank_select(_dot_split(psum, pair_ref[...]), blk, qpos[0:TPAD],
                                                     NSA_TOPN - 1, 1)

    pipe.run(lambda g: 0, body, finish)


def _nsa_cmp_decode(page_table, qn, pool, pair, cache_t, layer, npg, nbt, past_len):
    nb, n_pages = page_table.shape
    d = NSA_D
    out = [jax.ShapeDtypeStruct((nb, NSA_H * TPAD, d), F32), jax.ShapeDtypeStruct((nb, TPAD, past_len // NSA_SEL), F32)]
    cm = pltpu.VMEM((nbt, n_pages // npg, 2 * d, npg * cache_t.shape[3] // NSA_CMP), F32)
    return _paged_call(functools.partial(_nsa_cmp_decode_kernel, npg=npg, nbt=nbt, layer=layer, past_len=past_len),
                       "nsa_compressed_decode", page_table, [qn, pool, pair], cache_t, out, 2 * d, npg, nbt, [cm])


def _nsa_sel_decode_kernel(pt_ref, q_ref, sel_ref, kext_ref, cache_ref, m_ref, l_ref, acc_ref, buf, sem, *,
                           npg, nbt, layer):
    nb, n_pages = pt_ref.shape
    page = buf.shape[-1]
    d = NSA_D
    rows = NSA_H * TPAD
    pipe = _PagePipe(cache_ref, pt_ref, buf, sem, layer, 2 * d, 2 * d, npg, nbt, n_pages // npg, nb // nbt)
    slope = _alibi_rows(rows, TPAD)

    def body(g, k, slot, carry):
        tile_bias = slope * jnp.asarray(k * (npg * page), F32)
        new = []
        for bi in range(nbt):
            b = g * nbt + bi
            sel_bias = jnp.concatenate([(sel_ref[b, k] - 1.0) * BIAS_BIG] * NSA_H, axis=0)
            q_ext = _nsa_query_ext(_stack_heads(q_ref[b], d), slope, sel_bias, tile_bias)
            pages = [buf[slot, bi * npg + jj].astype(BF16) for jj in range(npg)]
            s = jnp.concatenate([_dot(q_ext, jnp.concatenate([pages[jj][0:d, :], kext_ref[jj]], axis=0))
                                 for jj in range(npg)], axis=-1)

            def pv(p, pages=pages):
                out = _dot_nt(p[:, 0:page], pages[0][d:2 * d, :])
                for jj in range(1, npg):
                    out = out + _dot_nt(p[:, jj * page:(jj + 1) * page], pages[jj][d:2 * d, :])
                return out

            new.append(_softmax_step(s, None, *carry[bi], pv))
        return tuple(new)

    def finish(g, carry):
        for bi in range(nbt):
            m_ref[g * nbt + bi], l_ref[g * nbt + bi], acc_ref[g * nbt + bi] = carry[bi]

    pipe.run(lambda g: tuple(_softmax_init(rows, d) for _ in range(nbt)), body, finish)


def _nsa_finish_decode_kernel(q_ref, m_ref, l_ref, acc_ref, oc_ref, rnew_ref, wpast_ref, wnew_ref, gate_ref, o_ref, *,
                              nbt, past_len, l_new):
    d = NSA_D
    rows = NSA_H * TPAD
    slope = _alibi_rows(rows, TPAD)
    tq = lax.rem(lax.broadcasted_iota(jnp.int32, (rows, 1), 0), TPAD)
    tk = lax.broadcasted_iota(jnp.int32, (rows, TPAD), 1)
    new_mask = (tk <= tq) & (tk < l_new)
    new_bias = slope * (tq - tk).astype(F32)
    for bi in range(nbt):
        q = _stack_heads(q_ref[bi], d).astype(BF16)
        rnew = rnew_ref[bi].astype(BF16)
        sn = _dot_nt(q, rnew[:, 2 * d:3 * d]) + slope * (past_len + tk).astype(F32)
        mm, ll, aa = _softmax_step(sn, new_mask, m_ref[bi], l_ref[bi], acc_ref[bi],
                                   lambda p, rnew=rnew: _dot(p, rnew[:, 3 * d:4 * d]))
        o_s = aa / jnp.maximum(ll, TINY)
        wpast = wpast_ref[bi].astype(BF16)
        wlen = wpast.shape[1]
        dw = tq + (wlen - lax.broadcasted_iota(jnp.int32, (rows, wlen), 1))
        sw = _dot(q, wpast[0:d, :]) - slope * dw.astype(F32)
        mm, ll, aa = _softmax_step(sw, dw < NSA_WINDOW, *_softmax_init(rows, d),
                                   lambda p, wpast=wpast: _dot_nt(p, wpast[d:2 * d, :]))
        wnew = wnew_ref[bi].astype(BF16)
        swn = _dot_nt(q, wnew[:, 0:d]) - new_bias
        mm, ll, aa = _softmax_step(swn, new_mask, mm, ll, aa, lambda p, wnew=wnew: _dot(p, wnew[:, d:2 * d]))
        o_w = aa / jnp.maximum(ll, TINY)
        o_ref[bi] = _nsa_combine(gate_ref[bi], oc_ref[bi], o_s, o_w, TPAD)


def _nsa_sel_decode(page_table, qn, sel_steps, kext, o_c, rows_new, win_past_t, win_new, gates, cache_t, layer, npg,
                    nbt, past_len, l_new):
    nb = page_table.shape[0]
    d = NSA_D
    rows = NSA_H * TPAD
    wlen = win_past_t.shape[2]
    stats = [jax.ShapeDtypeStruct((nb, rows, 1), F32), jax.ShapeDtypeStruct((nb, rows, 1), F32),
             jax.ShapeDtypeStruct((nb, rows, d), F32)]
    m, l, acc = _paged_call(functools.partial(_nsa_sel_decode_kernel, npg=npg, nbt=nbt, layer=layer),
                            "nsa_selected_decode", page_table, [qn, sel_steps, kext], cache_t, stats, 2 * d, npg, nbt)
    per_b = lambda a, n: pl.BlockSpec((nbt, a, n), lambda b: (b, 0, 0))
    return pl.pallas_call(
        functools.partial(_nsa_finish_decode_kernel, nbt=nbt, past_len=past_len, l_new=l_new),
        out_shape=jax.ShapeDtypeStruct((nb, TPAD, NSA_H * d), F32),
        grid=(nb // nbt,),
        in_specs=[per_b(TPAD, NSA_H * d), per_b(rows, 1), per_b(rows, 1), per_b(rows, d), per_b(rows, d),
                  per_b(TPAD, 4 * d), per_b(2 * d, wlen), per_b(TPAD, 2 * d), per_b(TPAD, V7X_LANES)],
        out_specs=per_b(TPAD, NSA_H * d),
        compiler_params=_cparams("parallel"),
        name="nsa_window_finish_decode",
    )(qn, m, l, acc, o_c, rows_new, win_past_t, win_new, gates)


def _out_proj_kernel(x_ref, olat_ref, ohg_ref, onsa_ref, os5_ref, wuv_ref, wout_ref, g1_ref, sc2_ref, sh2_ref,
                     gffn_ref, x1_ref, h2_ref):
    w = NSA_H * NSA_D
    omla = _dot(olat_ref[...].astype(BF16), wuv_ref[...])
    mix = (_dot(omla.astype(BF16), wout_ref[0:w, :]) + _dot(ohg_ref[...].astype(BF16), wout_ref[w:2 * w, :])
           + _dot(onsa_ref[...].astype(BF16), wout_ref[2 * w:3 * w, :])
           + _dot(os5_ref[...].astype(BF16), wout_ref[3 * w:4 * w, :]))
    x1 = x_ref[...] + g1_ref[...] * mix
    x1_ref[...] = x1
    var = jnp.mean(x1 * x1, axis=-1, keepdims=True)
    h2 = (x1 * lax.rsqrt(var + EPS) * gffn_ref[...]) * (1.0 + sc2_ref[...]) + sh2_ref[...]
    h2_ref[...] = h2.astype(BF16)


def _out_proj(x2d, olat, ohg, onsa, os5, wuv_bd, wout, g1, sc2, sh2, g_ffn, rows_per_seq, tm):
    r, d = x2d.shape
    if rows_per_seq is not None:
        tps = rows_per_seq // tm
        mod_spec = pl.BlockSpec((None, 1, d), lambda i: (i // tps, 0, 0))
    else:
        mod_spec = pl.BlockSpec((tm, d), lambda i: (i, 0))
    row = lambda n: pl.BlockSpec((tm, n), lambda i: (i, 0))
    const = lambda a, b: pl.BlockSpec((a, b), lambda i: (0, 0))
    w = NSA_H * NSA_D
    return pl.pallas_call(
        _out_proj_kernel,
        out_shape=[jax.ShapeDtypeStruct((r, d), F32), jax.ShapeDtypeStruct((r, d), BF16)],
        grid=(r // tm,),
        in_specs=[row(d), row(MLA_H * MLA_KVLORA), row(w), row(w), row(w), const(MLA_H * MLA_KVLORA, w),
                  const(4 * w, d), mod_spec, mod_spec, mod_spec, const(1, d)],
        out_specs=[row(d), row(d)],
        compiler_params=_cparams("parallel"),
        name="out_proj_residual",
    )(x2d, olat, ohg, onsa, os5, wuv_bd, wout, g1, sc2, sh2, g_ffn)


def _ffn_tail(j, upg_scr, upv_scr, offs, tm, cwg_ref, cwv_ref, cbg_ref, cbv_ref, wd_ref, x1_ref, g2_ref, o_ref,
              acc_scr):
    g = cbg_ref[...]
    v = cbv_ref[...]
    for k in range(CONV_W):
        g = g + upg_scr[pl.ds(offs[k], tm), :] * cwg_ref[k:k + 1, :]
        v = v + upv_scr[pl.ds(offs[k], tm), :] * cwv_ref[k:k + 1, :]
    part = _dot((_silu(g) * v).astype(BF16), wd_ref[...])

    @pl.when(j == 0)
    def _():
        acc_scr[...] = part

    @pl.when(j > 0)
    def _():
        acc_scr[...] = acc_scr[...] + part

    @pl.when(j == pl.num_programs(1) - 1)
    def _():
        o_ref[...] = x1_ref[...] + g2_ref[...] * acc_scr[...]


def _ffn_seq_kernel(h2_ref, halo_ref, x1_ref, g2_ref, wg_ref, wv_ref, cwg_ref, cwv_ref, cbg_ref, cbv_ref, wd_ref,
                    o_ref, sg_ref, sv_ref, upg_scr, upv_scr, acc_scr, *, tm, tiles_per_seq, halo):
    i, j = pl.program_id(0), pl.program_id(1)
    he = jnp.concatenate([halo_ref[...], h2_ref[...]], axis=0)
    rowid = lax.broadcasted_iota(jnp.int32, (halo + tm, 1), 0)
    keep = jnp.logical_or(rowid >= halo, lax.rem(i, tiles_per_seq) != 0)
    upg_scr[...] = jnp.where(keep, _dot(he, wg_ref[...]), 0.0)
    upv_scr[...] = jnp.where(keep, _dot(he, wv_ref[...]), 0.0)
    sg_ref[...] = upg_scr[pl.ds(halo + tm - (CONV_W - 1), CONV_W - 1), :]
    sv_ref[...] = upv_scr[pl.ds(halo + tm - (CONV_W - 1), CONV_W - 1), :]
    offs = tuple(halo - (CONV_W - 1) + k for k in range(CONV_W))
    _ffn_tail(j, upg_scr, upv_scr, offs, tm, cwg_ref, cwv_ref, cbg_ref, cbv_ref, wd_ref, x1_ref, g2_ref, o_ref,
              acc_scr)


def _ffn_seq(h2, x1, g2, wg, wv, cwg, cwv, cbg, cbv, wd, rows_per_seq, tm, ck):
    r, d = x1.shape
    dff = wd.shape[0]
    nj = dff // ck
    tps = rows_per_seq // tm
    halo = 16
    hb = tm // halo
    xo, sg, sv = pl.pallas_call(
        functools.partial(_ffn_seq_kernel, tm=tm, tiles_per_seq=tps, halo=halo),
        out_shape=[jax.ShapeDtypeStruct((r, d), F32),
                   jax.ShapeDtypeStruct((r // tm, CONV_W - 1, dff), F32),
                   jax.ShapeDtypeStruct((r // tm, CONV_W - 1, dff), F32)],
        grid=(r // tm, nj),
        in_specs=[pl.BlockSpec((tm, d), lambda i, j: (i, 0)),
                  pl.BlockSpec((halo, d), lambda i, j: (jnp.maximum(i * hb - 1, 0), 0)),
                  pl.BlockSpec((tm, d), lambda i, j: (i, 0)),
                  pl.BlockSpec((None, 1, d), lambda i, j: (i // tps, 0, 0)),
                  pl.BlockSpec((d, ck), lambda i, j: (0, j)),
                  pl.BlockSpec((d, ck), lambda i, j: (0, j)),
                  pl.BlockSpec((CONV_W, ck), lambda i, j: (0, j)),
                  pl.BlockSpec((CONV_W, ck), lambda i, j: (0, j)),
                  pl.BlockSpec((1, ck), lambda i, j: (0, j)),
                  pl.BlockSpec((1, ck), lambda i, j: (0, j)),
                  pl.BlockSpec((ck, d), lambda i, j: (j, 0))],
        out_specs=[pl.BlockSpec((tm, d), lambda i, j: (i, 0)),
                   pl.BlockSpec((None, CONV_W - 1, ck), lambda i, j: (i, 0, j)),
                   pl.BlockSpec((None, CONV_W - 1, ck), lambda i, j: (i, 0, j))],
        scratch_shapes=[pltpu.VMEM((halo + tm, ck), F32), pltpu.VMEM((halo + tm, ck), F32),
                        pltpu.VMEM((tm, d), F32)],
        compiler_params=_cparams("parallel", "arbitrary"),
        name="ffn_conv_prompt",
    )(h2, h2, x1, g2, wg, wv, cwg, cwv, cbg, cbv, wd)
    return xo, sg[tps - 1::tps], sv[tps - 1::tps]


def _ffn_step_kernel(h2_ref, sg_in_ref, sv_in_ref, x1_ref, g2_ref, wg_ref, wv_ref, cwg_ref, cwv_ref, cbg_ref, cbv_ref,
                     wd_ref, o_ref, sg_ref, sv_ref, upg_scr, upv_scr, acc_scr, *, tm, nb):
    j = pl.program_id(1)
    hist = (CONV_W - 1) * nb
    upg_scr[0:hist, :] = sg_in_ref[...]
    upv_scr[0:hist, :] = sv_in_ref[...]
    upg_scr[hist:hist + tm, :] = _dot(h2_ref[...], wg_ref[...])
    upv_scr[hist:hist + tm, :] = _dot(h2_ref[...], wv_ref[...])
    sg_ref[...] = upg_scr[tm:tm + hist, :]
    sv_ref[...] = upv_scr[tm:tm + hist, :]
    offs = tuple(k * nb for k in range(CONV_W))
    _ffn_tail(j, upg_scr, upv_scr, offs, tm, cwg_ref, cwv_ref, cbg_ref, cbv_ref, wd_ref, x1_ref, g2_ref, o_ref,
              acc_scr)


def _ffn_step(h2, sg_in, sv_in, x1, g2, wg, wv, cwg, cwv, cbg, cbv, wd, nb, ck):
    tm, d = x1.shape
    dff = wd.shape[0]
    nj = dff // ck
    hist = (CONV_W - 1) * nb
    full = lambda a, b: pl.BlockSpec((a, b), lambda i, j: (0, 0))
    colblk = lambda a: pl.BlockSpec((a, ck), lambda i, j: (0, j))
    return pl.pallas_call(
        functools.partial(_ffn_step_kernel, tm=tm, nb=nb),
        out_shape=[jax.ShapeDtypeStruct((tm, d), F32), jax.ShapeDtypeStruct((hist, dff), F32),
                   jax.ShapeDtypeStruct((hist, dff), F32)],
        grid=(1, nj),
        in_specs=[full(tm, d), colblk(hist), colblk(hist), full(tm, d), full(tm, d), colblk(d), colblk(d),
                  colblk(CONV_W), colblk(CONV_W), colblk(1), colblk(1),
                  pl.BlockSpec((ck, d), lambda i, j: (j, 0))],
        out_specs=[full(tm, d), colblk(hist), colblk(hist)],
        scratch_shapes=[pltpu.VMEM((hist + tm, ck), F32), pltpu.VMEM((hist + tm, ck), F32),
                        pltpu.VMEM((tm, d), F32)],
        compiler_params=_cparams("parallel", "arbitrary"),
        name="ffn_conv_step",
    )(h2, sg_in, sv_in, x1, g2, wg, wv, cwg, cwv, cbg, cbv, wd)


def _final_norm_kernel(x_ref, g_ref, o_ref):
    x = x_ref[...]
    var = jnp.mean(x * x, axis=-1, keepdims=True)
    o_ref[...] = x * lax.rsqrt(var + EPS) * g_ref[...]


def _final_norm(x2d, g, tm):
    r, d = x2d.shape
    return pl.pallas_call(
        _final_norm_kernel,
        out_shape=jax.ShapeDtypeStruct((r, d), F32),
        grid=(r // tm,),
        in_specs=[pl.BlockSpec((tm, d), lambda i: (i, 0)), pl.BlockSpec((1, d), lambda i: (0, 0))],
        out_specs=pl.BlockSpec((tm, d), lambda i: (i, 0)),
        compiler_params=_cparams("parallel"),
        name="final_rmsnorm",
    )(x2d, g)


def _rope_tables(pos):
    half = MLA_ROPE // 2
    inv = ROPE_BASE ** (-jnp.arange(half, dtype=F32) / half)
    ang = pos.astype(F32)[:, None] * inv[None, :]
    c, s = jnp.cos(ang), jnp.sin(ang)
    return jnp.concatenate([c, c], axis=-1), jnp.concatenate([-s, s], axis=-1)


def _swap_halves(w):
    half = w.shape[-1] // 2
    return jnp.concatenate([w[..., half:], w[..., :half]], axis=-1)


def _block_diag(blocks):
    g, a, b = blocks.shape
    eye = jnp.eye(g, dtype=blocks.dtype)
    return (blocks[:, :, None, :] * eye[:, None, :, None]).reshape(g * a, g * b)


def _time_major_perm(bt, tl):
    n = bt * tl
    src = jnp.arange(n)
    dst = (src % tl) * bt + src // tl
    return (dst[:, None] == jnp.arange(n)[None, :]).astype(BF16).T


def _nsa_key_ext(pos, block):
    lane = jnp.arange(NSA_XW)[None, :]
    onehot = (lane == block[:, None]) & (lane < NSA_X_BLOCKS)
    ext = (onehot.astype(F32) + jnp.where(lane == NSA_X_HI, (pos // V7X_LANES)[:, None], 0)
           + jnp.where(lane == NSA_X_LO, (pos % V7X_LANES)[:, None], 0) + (lane == NSA_X_ONE))
    return ext.astype(BF16)


def _layer_consts(l, w_in, w_out, mla_w_uq, mla_w_uk, mla_w_uv, s5_b_re, s5_b_im, s5_c_re, s5_c_im, ffn_w_up,
                  ffn_w_down):
    d = w_in.shape[1]
    w = w_in[l]
    o = 0
    cq = w[:, o:o + MLA_QLORA]; o += MLA_QLORA
    ckv = w[:, o:o + MLA_KVLORA]; o += MLA_KVLORA
    kr = w[:, o:o + MLA_ROPE]; o += MLA_ROPE
    hgw = w[:, o:o + HG_COLS]; o += HG_COLS
    qn = w[:, o:o + NSA_H * NSA_D]; o += NSA_H * NSA_D
    kvs = w[:, o:o + 6 * NSA_D]; o += 6 * NSA_D
    gates = w[:, o:o + 3 * NSA_H]; o += 3 * NSA_H
    s5w = w[:, o:o + S5_COLS]
    pad = jnp.zeros((d, V7X_LANES - 2 * MLA_ROPE - 3 * NSA_H), w.dtype)
    w_perm = jnp.concatenate([cq, ckv, kr, _swap_halves(kr), gates, pad, hgw, qn, kvs, s5w], axis=1).astype(BF16)
    uq = mla_w_uq[l].reshape(MLA_QLORA, MLA_H, MLA_NOPE + MLA_ROPE)
    uq_rope = uq[:, :, MLA_NOPE:]
    wuq_perm = jnp.concatenate([uq[:, :, :MLA_NOPE].reshape(MLA_QLORA, -1), uq_rope.reshape(MLA_QLORA, -1),
                                _swap_halves(uq_rope).reshape(MLA_QLORA, -1)], axis=1).astype(BF16)
    return dict(
        w_perm=w_perm, wuq_perm=wuq_perm,
        wuk_bd=_block_diag(mla_w_uk[l].transpose(1, 2, 0)).astype(BF16),
        wuv_bd=_block_diag(mla_w_uv[l].transpose(1, 0, 2)).astype(BF16),
        wout=w_out[l].astype(BF16),
        br_bd=_block_diag(s5_b_re[l].transpose(0, 2, 1)).astype(BF16),
        bi_bd=_block_diag(s5_b_im[l].transpose(0, 2, 1)).astype(BF16),
        cre_bd=_block_diag(s5_c_re[l].transpose(0, 2, 1)).astype(BF16),
        cim_bd=_block_diag(s5_c_im[l].transpose(0, 2, 1)).astype(BF16),
        wg=ffn_w_up[l][:, :ffn_w_down.shape[1]].astype(BF16),
        wv=ffn_w_up[l][:, ffn_w_down.shape[1]:].astype(BF16),
        wd=ffn_w_down[l].astype(BF16),
    )


def _hgrn_state_to_bd(s):
    b = s.shape[0]
    blocks = s.transpose(0, 1, 3, 2)
    eye = jnp.eye(HG_H, dtype=s.dtype)
    return (blocks[:, :, :, None, :] * eye[None, :, None, :, None]).reshape(b, HG_H * HG_DV, HG_H * HG_DK)


def _hgrn_state_from_bd(st):
    b = st.shape[0]
    st5 = st.reshape(b, HG_H, HG_DV, HG_H, HG_DK)
    return jnp.stack([st5[:, h, :, h, :] for h in range(HG_H)], axis=1).transpose(0, 1, 3, 2)


def _pad_t(x):
    return jnp.pad(x, ((0, 0), (0, TPAD - x.shape[1]), (0, 0)))


def kernel(x_prompt, x_sample, c_prompt, c_sample, cache_mla, cache_nsa, cache_nsa_win, state_hgrn, state_s5_re, state_s5_im, state_ffn_conv, page_table, w_ada, b_ada, g_mix, g_ffn, w_in, w_out, mla_g_q, mla_g_kv, mla_w_uq, mla_w_uk, mla_w_uv, hg_lb, hg_g_out, s5_a_re, s5_a_im, s5_log_dt, s5_b_re, s5_b_im, s5_c_re, s5_c_im, s5_d, s5_w_glu, s5_b_glu, ffn_w_up, ffn_conv_w, ffn_conv_b, ffn_w_down, g_final):
    depth = w_in.shape[0]
    bp, lp, d = x_prompt.shape
    nb, ls, _ = x_sample.shape
    page = cache_mla.shape[2]
    n_pages = page_table.shape[1]
    past_len = n_pages * page
    dff = ffn_w_down.shape[1]
    rp, rs = bp * lp, nb * ls
    assert ls <= TPAD and page == V7X_LANES and lp % NSA_SEL == 0
    assert cache_nsa_win.shape[2] == NSA_WINDOW and past_len >= NSA_WINDOW

    tm_in, tm_row, tq_mla, tk_mla, tq_nsa, tk_nsa = 256, 512, 256, 256, 128, 512
    bt, tl_rec, c_hg = 8, 128, 16
    ck = 896
    npg, nbt = 16, 4
    npg_cmp, nbt_cmp = 32, 2
    assert (npg * page // NSA_SEL) == NSA_X_BLOCKS and npg_cmp * page // NSA_CMP == V7X_LANES

    ones_bd = _block_diag(jnp.ones((HG_H, HG_DK, HG_DV), BF16))
    cos_p, sin_p = _rope_tables(jnp.arange(lp))
    cos_s, sin_s = _rope_tables(past_len + jnp.arange(ls))
    cos_s, sin_s = jnp.tile(cos_s, (nb, 1)), jnp.tile(sin_s, (nb, 1))
    tile_h = lambda t: jnp.tile(t, (1, MLA_H))
    perm_p = _time_major_perm(bt, tl_rec)
    perm_s = _time_major_perm(bt, ls)
    ncmp_p, nsel_p = lp // NSA_CMP, lp // NSA_SEL
    cmp_per_sel = NSA_SEL // NSA_CMP
    pairt_p = (jnp.arange(nsel_p)[:, None] == jnp.arange(ncmp_p)[None, :] // cmp_per_sel).astype(BF16)
    kext_p = _nsa_key_ext(jnp.arange(lp), jnp.arange(lp) // NSA_SEL)
    eye_p = jnp.eye(nsel_p, NSA_X_BLOCKS, dtype=BF16)
    ncmp_s, nsel_s = past_len // NSA_CMP, past_len // NSA_SEL
    pair_s = (jnp.arange(ncmp_s)[:, None] // cmp_per_sel == jnp.arange(nsel_s)[None, :]).astype(BF16)
    tok = jnp.arange(npg * page)
    kext_s = _nsa_key_ext(tok, tok // NSA_SEL).reshape(npg, page, NSA_XW).transpose(0, 2, 1)
    ptok = jnp.arange(npg_cmp * page).reshape(npg_cmp, page)
    pool = ((ptok[:, :, None] // NSA_CMP == jnp.arange(V7X_LANES)[None, None, :]).astype(F32)
            * (1.0 / NSA_CMP)).astype(BF16)
    cache_mla_t = cache_mla.transpose(0, 1, 3, 2)
    cache_nsa_t = cache_nsa.transpose(0, 1, 3, 4, 2).reshape(cache_nsa.shape[0], cache_nsa.shape[1], 4 * NSA_D, page)
    win_t = cache_nsa_win.transpose(0, 1, 3, 4, 2).reshape(depth, nb, 2 * NSA_D, NSA_WINDOW)

    mod = _modulation(jnp.concatenate([c_prompt, c_sample], axis=0), w_ada, b_ada)
    xp = x_prompt.reshape(rp, d)
    xs = x_sample.reshape(rs, d)
    leaves_p, leaves_s = [], []
    row1 = lambda v: v.reshape(1, -1)
    for l in range(depth):
        cst = _layer_consts(l, w_in, w_out, mla_w_uq, mla_w_uk, mla_w_uv, s5_b_re, s5_b_im, s5_c_re, s5_c_im,
                            ffn_w_up, ffn_w_down)
        gout_row = row1(jnp.tile(hg_g_out[l], HG_H))
        are_row, aim_row = row1(s5_a_re[l]), row1(s5_a_im[l])
        ldt_row = row1(jnp.repeat(s5_log_dt[l], S5_N))
        d_row, bglu_row = row1(s5_d[l]), row1(s5_b_glu[l])
        wglu = s5_w_glu[l].astype(BF16)
        cwg, cwv = ffn_conv_w[l][:, :dff], ffn_conv_w[l][:, dff:]
        cbg, cbv = row1(ffn_conv_b[l][:dff]), row1(ffn_conv_b[l][dff:])
        s5_args = (cst["br_bd"], cst["bi_bd"], are_row, aim_row, ldt_row, cst["cre_bd"], cst["cim_bd"], d_row, wglu,
                   bglu_row)

        m6 = mod[l, :bp].reshape(bp, 6, 1, d)
        sh1, sc1, g1, sh2, sc2, g2 = (m6[:, i] for i in range(6))
        cq, mla_rows, hgc, qn, nsa_rows, win, gates, s5u = _proj_in(
            xp, sc1, sh1, row1(g_mix[l]), cst["w_perm"], row1(mla_g_kv[l]), cos_p, sin_p, lp, tm_in)
        qlat, qrope = _mla_q(cq, row1(mla_g_q[l]), cst["wuq_perm"], cst["wuk_bd"], tile_h(cos_p), tile_h(sin_p), lp,
                             tm_row)
        olat = _mla_prompt(qlat.reshape(bp, lp, -1), qrope.reshape(bp, lp, -1), mla_rows.reshape(bp, lp, -1),
                           tq_mla, tk_mla)
        ohg, st = _hgrn(hgc.reshape(bp, lp, -1), hg_lb, gout_row, ones_bd,
                        jnp.zeros((bp, HG_H * HG_DV, HG_H * HG_DK), F32), bt, tl_rec, c_hg, lp, l)
        onsa = _nsa_prompt(qn.reshape(bp, lp, -1), nsa_rows.reshape(bp, lp, -1), win.reshape(bp, lp, -1),
                           gates.reshape(bp, lp, -1), pairt_p, kext_p, eye_p, tq_nsa, tk_nsa)
        zero_state = jnp.zeros((bp, S5_STATE), F32)
        os5, s5r, s5i = _s5(s5u.reshape(bp, lp, -1), perm_p, perm_p.T, *s5_args, zero_state, zero_state, bt, tl_rec)
        x1, h2 = _out_proj(xp, olat.reshape(rp, -1), ohg.reshape(rp, -1), onsa.reshape(rp, -1), os5.reshape(rp, -1),
                           cst["wuv_bd"], cst["wout"], g1, sc2, sh2, row1(g_ffn[l]), lp, tm_row)
        xp, cg, cv = _ffn_seq(h2, x1, g2, cst["wg"], cst["wv"], cwg, cwv, cbg, cbv, cst["wd"], lp, tm_row, ck)
        wlen = min(NSA_WINDOW, lp)
        leaves_p.append(dict(
            mla=mla_rows.reshape(bp, lp, MLA_ROW), nsa=nsa_rows.reshape(bp, lp, 4, NSA_D),
            win=win.reshape(bp, lp, 2, NSA_D)[:, lp - wlen:], hgrn=_hgrn_state_from_bd(st),
            s5_re=s5r.reshape(bp, S5_G, S5_N), s5_im=s5i.reshape(bp, S5_G, S5_N),
            conv=jnp.concatenate([cg, cv], axis=-1)))

        m6 = jnp.repeat(mod[l, bp:].reshape(nb, 6, d), ls, axis=0)
        sh1, sc1, g1, sh2, sc2, g2 = (m6[:, i] for i in range(6))
        cq, mla_rows, hgc, qn, nsa_rows, win, gates, s5u = _proj_in(
            xs, sc1, sh1, row1(g_mix[l]), cst["w_perm"], row1(mla_g_kv[l]), cos_s, sin_s, None, rs)
        qlat, qrope = _mla_q(cq, row1(mla_g_q[l]), cst["wuq_perm"], cst["wuk_bd"], tile_h(cos_s), tile_h(sin_s),
                             None, rs)
        seq = lambda a: _pad_t(a.reshape(nb, ls, -1))
        olat = _mla_decode(page_table, seq(qlat), seq(qrope), seq(mla_rows), cache_mla_t, l, npg, nbt, ls)[:, :ls]
        ohg, st = _hgrn(seq(hgc), hg_lb, gout_row, ones_bd, _hgrn_state_to_bd(state_hgrn[l]), bt, TPAD, TPAD, ls, l)
        ohg = ohg[:, :ls]
        o_c, sel = _nsa_cmp_decode(page_table, seq(qn), pool, pair_s, cache_nsa_t, l, npg_cmp, nbt_cmp, past_len)
        sel_steps = sel.reshape(nb, TPAD, n_pages // npg, NSA_X_BLOCKS).transpose(0, 2, 1, 3)
        onsa = _nsa_sel_decode(page_table, seq(qn), sel_steps, kext_s, o_c, seq(nsa_rows), win_t[l], seq(win),
                               seq(gates), cache_nsa_t, l, npg, nbt, past_len, ls)[:, :ls]
        os5, s5r, s5i = _s5(s5u.reshape(nb, ls, -1), perm_s, perm_s.T, *s5_args,
                            state_s5_re[l].reshape(nb, S5_STATE), state_s5_im[l].reshape(nb, S5_STATE), bt, ls)
        x1, h2 = _out_proj(xs, olat.reshape(rs, -1), ohg.reshape(rs, -1), onsa.reshape(rs, -1), os5.reshape(rs, -1),
                           cst["wuv_bd"], cst["wout"], g1, sc2, sh2, row1(g_ffn[l]), None, rs)
        tmaj = lambda a: a.reshape(nb, ls, -1).transpose(1, 0, 2).reshape(rs, -1)
        conv_in = state_ffn_conv[l].transpose(1, 0, 2).reshape((CONV_W - 1) * nb, 2 * dff)
        xs_tm, sg, sv = _ffn_step(tmaj(h2), conv_in[:, :dff], conv_in[:, dff:], tmaj(x1), tmaj(g2), cst["wg"],
                                  cst["wv"], cwg, cwv, cbg, cbv, cst["wd"], nb, ck)
        xs = xs_tm.reshape(ls, nb, d).transpose(1, 0, 2).reshape(rs, d)
        conv_out = jnp.concatenate([sg, sv], axis=-1).reshape(CONV_W - 1, nb, 2 * dff).transpose(1, 0, 2)
        win_state = jnp.concatenate([cache_nsa_win[l], win.reshape(nb, ls, 2, NSA_D)], axis=1)[:, ls:]
        leaves_s.append(dict(
            mla=mla_rows.reshape(nb, ls, MLA_ROW), nsa=nsa_rows.reshape(nb, ls, 4, NSA_D), win=win_state,
            hgrn=_hgrn_state_from_bd(st), s5_re=s5r.reshape(nb, S5_G, S5_N), s5_im=s5i.reshape(nb, S5_G, S5_N),
            conv=conv_out))

    y_prompt = _final_norm(xp, row1(g_final), tm_row).reshape(bp, lp, d)
    y_sample = _final_norm(xs, row1(g_final), rs).reshape(nb, ls, d)
    stk = lambda lst, name: jnp.stack([dd[name] for dd in lst], axis=0)
    out = [y_prompt, y_sample]
    for name in ("mla", "nsa", "win", "hgrn", "s5_re", "s5_im", "conv"):
        out += [stk(leaves_p, name), stk(leaves_s, name)]
    return tuple(out)
```

```python
import functools

import jax
import jax.numpy as jnp
from jax import lax
from jax.experimental import pallas as pl
from jax.experimental.pallas import tpu as pltpu

F32 = jnp.float32
BF16 = jnp.bfloat16

MLA_H, MLA_NOPE, MLA_ROPE, MLA_V = 4, 64, 32, 64
MLA_QLORA, MLA_KVLORA = 384, 128
ROPE_BASE = 10000.0
HG_H, HG_DK, HG_DV = 4, 64, 64
NSA_H, NSA_D, NSA_CMP, NSA_SEL, NSA_TOPN, NSA_WINDOW = 4, 64, 32, 64, 16, 512
S5_CH, S5_G, S5_N = 16, 16, 64
CONV_W = 3
EPS = 1e-6
NEG = -1e30
TINY = 1e-30
BIAS_BIG = 2.0 ** 100

HG_COLS = 2 * HG_H * HG_DK + 2 * HG_H * HG_DV
S5_COLS = S5_G * S5_CH
MLA_ROW = MLA_KVLORA + MLA_ROPE
S5_STATE = S5_G * S5_N

V7X_LANES = 128
V7X_SUBLANES = 8
V7X_VMEM_LIMIT_BYTES = 56 * 1024 * 1024

PC_CQ = 0
PC_CKV = 384
PC_MISC = 512
PC_GATE_LANE = 64
PC_HG = 640
PC_QN = 1664
PC_NSA = 1920
PC_WIN = 2176
PC_S5 = 2304
PC_TOTAL = 2560

NSA_XW = 64
NSA_X_BLOCKS = 32
NSA_X_HI = 32
NSA_X_LO = 33
NSA_X_ONE = 34


def _cparams(*semantics):
    return pltpu.CompilerParams(dimension_semantics=tuple(semantics), vmem_limit_bytes=V7X_VMEM_LIMIT_BYTES)


def _dot(a, b):
    return jnp.dot(a, b, preferred_element_type=F32)


def _dot_nt(a, b):
    return lax.dot_general(a, b, (((1,), (1,)), ((), ())), preferred_element_type=F32)


def _dot_tn(a, b):
    return lax.dot_general(a, b, (((0,), (0,)), ((), ())), preferred_element_type=F32)


def _split(a):
    hi = a.astype(BF16)
    return hi, (a - hi.astype(F32)).astype(BF16)


def _dot_split(a, b_bf16):
    hi, lo = _split(a)
    return _dot(hi, b_bf16) + _dot(lo, b_bf16)


def _silu(x):
    return x * jax.nn.sigmoid(x)


def _mod_kernel(c_ref, w_ref, b_ref, o_ref):
    s = _silu(c_ref[...]).astype(BF16)
    o_ref[...] = _dot(s, w_ref[...].astype(BF16)) + b_ref[...]


def _modulation(c_all, w_ada, b_ada):
    depth, d, n = w_ada.shape
    bc = c_all.shape[0]
    tn = 1536
    return pl.pallas_call(
        _mod_kernel,
        out_shape=jax.ShapeDtypeStruct((depth, bc, n), F32),
        grid=(depth, n // tn),
        in_specs=[pl.BlockSpec((bc, d), lambda l, j: (0, 0)),
                  pl.BlockSpec((None, d, tn), lambda l, j: (l, 0, j)),
                  pl.BlockSpec((None, 1, tn), lambda l, j: (l, 0, j))],
        out_specs=pl.BlockSpec((None, bc, tn), lambda l, j: (l, 0, j)),
        compiler_params=_cparams("parallel", "parallel"),
        name="adaln_modulation",
    )(c_all, w_ada, b_ada.reshape(depth, 1, n))


def _proj_in_kernel(x_ref, sc_ref, sh_ref, g_ref, w_ref, gkv_ref, cos_ref, sin_ref,
                    cq_ref, mla_ref, hg_ref, qn_ref, nsa_ref, win_ref, gate_ref, s5_ref):
    x = x_ref[...]
    var = jnp.mean(x * x, axis=-1, keepdims=True)
    h = (x * lax.rsqrt(var + EPS) * g_ref[...]) * (1.0 + sc_ref[...]) + sh_ref[...]
    cols = _dot(h.astype(BF16), w_ref[...])
    cq_ref[...] = cols[:, PC_CQ:PC_CQ + MLA_QLORA]
    ckv = cols[:, PC_CKV:PC_CKV + MLA_KVLORA]
    kvar = jnp.mean(ckv * ckv, axis=-1, keepdims=True)
    mla_ref[:, 0:MLA_KVLORA] = ckv * lax.rsqrt(kvar + EPS) * gkv_ref[...]
    kr = cols[:, PC_MISC:PC_MISC + MLA_ROPE]
    kr_swapped = cols[:, PC_MISC + MLA_ROPE:PC_MISC + 2 * MLA_ROPE]
    mla_ref[:, MLA_KVLORA:MLA_ROW] = kr * cos_ref[...] + kr_swapped * sin_ref[...]
    gate_ref[...] = jax.nn.sigmoid(cols[:, PC_MISC:PC_MISC + V7X_LANES])
    hg_ref[...] = cols[:, PC_HG:PC_HG + HG_COLS]
    qn_ref[...] = cols[:, PC_QN:PC_QN + NSA_H * NSA_D] * (NSA_D ** -0.5)
    nsa_ref[...] = cols[:, PC_NSA:PC_NSA + 4 * NSA_D]
    win_ref[...] = cols[:, PC_WIN:PC_WIN + 2 * NSA_D]
    s5_ref[...] = cols[:, PC_S5:PC_S5 + S5_COLS]


def _proj_in(x2d, scale, shift, g_mix, w_perm, g_kv, cos_t, sin_t, rows_per_seq, tm):
    r, d = x2d.shape
    if rows_per_seq is not None:
        tps = rows_per_seq // tm
        mod_spec = pl.BlockSpec((None, 1, d), lambda i: (i // tps, 0, 0))
        rope_spec = pl.BlockSpec((tm, MLA_ROPE), lambda i: (i % tps, 0))
    else:
        mod_spec = pl.BlockSpec((tm, d), lambda i: (i, 0))
        rope_spec = pl.BlockSpec((tm, MLA_ROPE), lambda i: (i, 0))
    row = lambda n: pl.BlockSpec((tm, n), lambda i: (i, 0))
    const = lambda a, b: pl.BlockSpec((a, b), lambda i: (0, 0))
    widths = (MLA_QLORA, MLA_ROW, HG_COLS, NSA_H * NSA_D, 4 * NSA_D, 2 * NSA_D, V7X_LANES, S5_COLS)
    return pl.pallas_call(
        _proj_in_kernel,
        out_shape=[jax.ShapeDtypeStruct((r, n), F32) for n in widths],
        grid=(r // tm,),
        in_specs=[row(d), mod_spec, mod_spec, const(1, d), const(d, PC_TOTAL), const(1, MLA_KVLORA),
                  rope_spec, rope_spec],
        out_specs=[row(n) for n in widths],
        compiler_params=_cparams("parallel"),
        name="proj_in",
    )(x2d, scale, shift, g_mix, w_perm, g_kv, cos_t, sin_t)


def _mla_q_kernel(cq_ref, g_ref, wuq_ref, wuk_ref, cos_ref, sin_ref, qlat_ref, qrope_ref):
    cq = cq_ref[...]
    var = jnp.mean(cq * cq, axis=-1, keepdims=True)
    c = cq * lax.rsqrt(var + EPS) * g_ref[...]
    q = _dot(c.astype(BF16), wuq_ref[...])
    scale = (MLA_NOPE + MLA_ROPE) ** -0.5
    n_nope = MLA_H * MLA_NOPE
    n_rope = MLA_H * MLA_ROPE
    qlat_ref[...] = _dot(q[:, :n_nope].astype(BF16), wuk_ref[...]) * scale
    qrope_ref[...] = (q[:, n_nope:n_nope + n_rope] * cos_ref[...]
                      + q[:, n_nope + n_rope:] * sin_ref[...]) * scale


def _mla_q(cq2d, g_q, wuq_perm, wuk_bd, cos_t, sin_t, rows_per_seq, tm):
    r = cq2d.shape[0]
    n_rope = MLA_H * MLA_ROPE
    if rows_per_seq is not None:
        tps = rows_per_seq // tm
        rope_spec = pl.BlockSpec((tm, n_rope), lambda i: (i % tps, 0))
    else:
        rope_spec = pl.BlockSpec((tm, n_rope), lambda i: (i, 0))
    row = lambda n: pl.BlockSpec((tm, n), lambda i: (i, 0))
    const = lambda a, b: pl.BlockSpec((a, b), lambda i: (0, 0))
    n_lat = MLA_H * MLA_KVLORA
    return pl.pallas_call(
        _mla_q_kernel,
        out_shape=[jax.ShapeDtypeStruct((r, n_lat), F32), jax.ShapeDtypeStruct((r, n_rope), F32)],
        grid=(r // tm,),
        in_specs=[row(MLA_QLORA), const(1, MLA_QLORA), const(MLA_QLORA, wuq_perm.shape[1]),
                  const(MLA_H * MLA_NOPE, n_lat), rope_spec, rope_spec],
        out_specs=[row(n_lat), row(n_rope)],
        compiler_params=_cparams("parallel"),
        name="mla_q_prep",
    )(cq2d, g_q, wuq_perm, wuk_bd, cos_t, sin_t)


def _softmax_step(s, mask, m, l, acc, pv):
    if mask is not None:
        s = jnp.where(mask, s, NEG)
    m_new = jnp.maximum(m, jnp.max(s, axis=-1, keepdims=True))
    alpha = jnp.exp(m - m_new)
    p = jnp.exp(s - m_new)
    if mask is not None:
        p = jnp.where(mask, p, 0.0)
    l_new = alpha * l + jnp.sum(p, axis=-1, keepdims=True)
    acc_new = alpha * acc + pv(p.astype(BF16))
    return m_new, l_new, acc_new


def _softmax_step_sum(s, mask, m, acc, pv):
    if mask is not None:
        s = jnp.where(mask, s, NEG)
    m_new = jnp.maximum(m, jnp.max(s, axis=-1, keepdims=True))
    p = jnp.exp(s - m_new)
    if mask is not None:
        p = jnp.where(mask, p, 0.0)
    return m_new, jnp.exp(m - m_new) * acc + pv(p.astype(BF16))


def _softmax_init(rows, dv):
    return jnp.full((rows, 1), NEG, F32), jnp.zeros((rows, 1), F32), jnp.zeros((rows, dv), F32)


def _stack_mla_q(qlat, qrope):
    parts = []
    for h in range(MLA_H):
        parts.append(jnp.concatenate([qlat[:, h * MLA_KVLORA:(h + 1) * MLA_KVLORA],
                                      qrope[:, h * MLA_ROPE:(h + 1) * MLA_ROPE]], axis=-1))
    return jnp.concatenate(parts, axis=0).astype(BF16)


def _mla_prompt_kernel(qlat_ref, qrope_ref, keys_ref, o_ref, *, tq, tk):
    i = pl.program_id(1)
    q = _stack_mla_q(qlat_ref[...], qrope_ref[...])
    rows = MLA_H * tq

    def chunk(c, carry, mask):
        start = pl.multiple_of(c * tk, tk)
        kblk = keys_ref[pl.ds(start, tk), :].astype(BF16)
        return _softmax_step(_dot_nt(q, kblk), mask, *carry, lambda p: _dot(p, kblk[:, :MLA_KVLORA]))

    diag = (i * tq) // tk
    carry = lax.fori_loop(0, diag, lambda c, carry: chunk(c, carry, None), _softmax_init(rows, MLA_KVLORA))
    qofs = i * tq - diag * tk + lax.rem(lax.broadcasted_iota(jnp.int32, (rows, tk), 0), tq)
    m, l, acc = chunk(diag, carry, lax.broadcasted_iota(jnp.int32, (rows, tk), 1) <= qofs)
    o = acc / jnp.maximum(l, TINY)
    o_ref[...] = jnp.concatenate([o[h * tq:(h + 1) * tq] for h in range(MLA_H)], axis=-1)


def _mla_prompt(qlat, qrope, keys, tq, tk):
    b, l, _ = qlat.shape
    n_lat = MLA_H * MLA_KVLORA
    assert tk % tq == 0
    return pl.pallas_call(
        functools.partial(_mla_prompt_kernel, tq=tq, tk=tk),
        out_shape=jax.ShapeDtypeStruct((b, l, n_lat), F32),
        grid=(b, l // tq),
        in_specs=[pl.BlockSpec((None, tq, n_lat), lambda bi, i: (bi, i, 0)),
                  pl.BlockSpec((None, tq, MLA_H * MLA_ROPE), lambda bi, i: (bi, i, 0)),
                  pl.BlockSpec((None, l, MLA_ROW), lambda bi, i: (bi, 0, 0))],
        out_specs=pl.BlockSpec((None, tq, n_lat), lambda bi, i: (bi, i, 0)),
        compiler_params=_cparams("parallel", "parallel"),
        name="mla_prompt_attention",
    )(qlat, qrope, keys)


def _hgrn_kernel(hg_ref, lb_ref, gout_ref, ones_ref, s0_ref, o_ref, st_ref,
                 st_scr, qe_scr, ke_scr, v_scr, dl_scr, o_scr, *, bt, tl, c, l_valid, layer):
    j = pl.program_id(1)
    nch = tl // c
    rows = bt * tl
    g = rows // c
    width = HG_H * HG_DK

    @pl.when(j == 0)
    def _():
        st_scr[...] = s0_ref[...]

    x = hg_ref[...].reshape(rows, HG_COLS)
    hq = x[:, 0:width]
    z = x[:, width:2 * width]
    vi = x[:, 2 * width:3 * width]
    hgate = x[:, 3 * width:4 * width]
    raw = lb_ref[...]
    e = jnp.exp(raw - jnp.max(raw, axis=0, keepdims=True))
    lbs = e / jnp.sum(e, axis=0, keepdims=True)
    lb = jnp.zeros((1, width), F32)
    for dd in range(1, layer + 1):
        lb = lb + lbs[dd:dd + 1, :]
    t_in = lax.rem(lax.broadcasted_iota(jnp.int32, (rows, width), 0), c)
    valid = (j * tl + lax.rem(lax.broadcasted_iota(jnp.int32, (rows, width), 0), tl)) < l_valid
    logf = jnp.where(valid, jnp.log(lb + (1.0 - lb) * jax.nn.sigmoid(z)), 0.0)
    kg = jnp.where(valid, (1.0 - lb) * jax.nn.sigmoid(-z), 0.0)
    qg = _silu(hq)
    b = logf
    k = 1
    while k < c:
        b = b + jnp.where(t_in >= k, pltpu.roll(b, k, 0), 0.0)
        k *= 2
    b3 = b.reshape(g, c, width)
    q3 = qg.reshape(g, c, width)
    k3 = kg.reshape(g, c, width)
    v3 = vi.reshape(g, c, width)
    ones_bd = ones_ref[...]
    tt = lax.broadcasted_iota(jnp.int32, (g, c, width), 1)
    o = jnp.zeros((g, c, width), F32)
    for s in range(c):
        bs = b3[:, s:s + 1, :]
        w = q3 * k3[:, s:s + 1, :] * jnp.exp(jnp.where(tt >= s, b3 - bs, NEG))
        r = _dot(w.reshape(rows, width).astype(BF16), ones_bd)
        o = o + r.reshape(g, c, width) * v3[:, s:s + 1, :]
    bl = b3[:, c - 1:c, :]
    o_scr[...] = o.reshape(bt, nch, c, width)
    qe_scr[...] = (q3 * jnp.exp(b3)).reshape(bt, nch, c, width).astype(BF16)
    ke_scr[...] = (k3 * jnp.exp(bl - b3)).reshape(bt, nch, c, width).astype(BF16)
    v_scr[...] = v3.reshape(bt, nch, c, width).astype(BF16)
    dl_scr[...] = jnp.exp(bl).reshape(bt, nch, 1, width)
    bd_mask = ones_bd.astype(F32)

    def chunk(n, carry):
        for bi in range(bt):
            st = st_scr[bi]
            o_scr[bi, n] = o_scr[bi, n] + _dot_nt(qe_scr[bi, n], st.astype(BF16))
            upd = _dot_tn(v_scr[bi, n], ke_scr[bi, n])
            st_scr[bi] = st * dl_scr[bi, n] + upd * bd_mask
        return carry

    lax.fori_loop(0, nch, chunk, 0)
    oo = o_scr[...].reshape(rows, width)
    ms = _dot_split(oo * oo, ones_bd) * (1.0 / HG_DV)
    on = oo * lax.rsqrt(ms + EPS) * gout_ref[...]
    o_ref[...] = (on * _silu(hgate)).reshape(bt, tl, width)

    @pl.when(j == pl.num_programs(1) - 1)
    def _():
        st_ref[...] = st_scr[...]


def _hgrn(hgc, hg_lb, gout_row, ones_bd, st0, bt, tl, c, l_valid, layer):
    b, l, _ = hgc.shape
    width = HG_H * HG_DK
    nch = tl // c
    return pl.pallas_call(
        functools.partial(_hgrn_kernel, bt=bt, tl=tl, c=c, l_valid=l_valid, layer=layer),
        out_shape=[jax.ShapeDtypeStruct((b, l, width), F32), jax.ShapeDtypeStruct((b, width, width), F32)],
        grid=(b // bt, l // tl),
        in_specs=[pl.BlockSpec((bt, tl, HG_COLS), lambda i, j: (i, j, 0)),
                  pl.BlockSpec(hg_lb.shape, lambda i, j: (0, 0)),
                  pl.BlockSpec((1, width), lambda i, j: (0, 0)),
                  pl.BlockSpec((width, width), lambda i, j: (0, 0)),
                  pl.BlockSpec((bt, width, width), lambda i, j: (i, 0, 0))],
        out_specs=[pl.BlockSpec((bt, tl, width), lambda i, j: (i, j, 0)),
                   pl.BlockSpec((bt, width, width), lambda i, j: (i, 0, 0))],
        scratch_shapes=[pltpu.VMEM((bt, width, width), F32),
                        pltpu.VMEM((bt, nch, c, width), BF16),
                        pltpu.VMEM((bt, nch, c, width), BF16),
                        pltpu.VMEM((bt, nch, c, width), BF16),
                        pltpu.VMEM((bt, nch, 1, width), F32),
                        pltpu.VMEM((bt, nch, c, width), F32)],
        compiler_params=_cparams("parallel", "arbitrary"),
        name="hgrn2_chunked",
    )(hgc, hg_lb, gout_row, ones_bd, st0)


def _s5_kernel(u_ref, perm_ref, permt_ref, br_ref, bi_ref, are_ref, aim_ref, ldt_ref, cre_ref, cim_ref, d_ref,
               wglu_ref, bglu_ref, x0r_ref, x0i_ref, y_ref, xr_out, xi_out,
               p_scr, q_scr, xr_scr, xi_scr, cr_scr, ci_scr, *, bt, tl):
    j = pl.program_id(1)
    n = bt * tl

    @pl.when(j == 0)
    def _():
        cr_scr[...] = x0r_ref[...]
        ci_scr[...] = x0i_ref[...]

    u = u_ref[...].reshape(n, S5_COLS)
    u_tm = _dot(perm_ref[...], u.astype(BF16)).astype(BF16)
    p_scr[...] = _dot(u_tm, br_ref[...])
    q_scr[...] = _dot(u_tm, bi_ref[...])
    ar, ai = are_ref[...], aim_ref[...]
    dt = jnp.exp(ldt_ref[...])
    mag = jnp.exp(dt * ar)
    abr1, abi1 = mag * jnp.cos(dt * ai), mag * jnp.sin(dt * ai)
    den = ar * ar + ai * ai
    fr1 = ((abr1 - 1.0) * ar + abi1 * ai) / den
    fi1 = (abi1 * ar - (abr1 - 1.0) * ai) / den
    shape = (bt, S5_STATE)
    abr, abi = jnp.broadcast_to(abr1, shape), jnp.broadcast_to(abi1, shape)
    fr, fi = jnp.broadcast_to(fr1, shape), jnp.broadcast_to(fi1, shape)

    def step(t, carry):
        xr, xi = carry
        off = pl.multiple_of(t * bt, bt)
        p = p_scr[pl.ds(off, bt), :]
        q = q_scr[pl.ds(off, bt), :]
        nxr = abr * xr - abi * xi + (fr * p - fi * q)
        nxi = abr * xi + abi * xr + (fr * q + fi * p)
        xr_scr[pl.ds(off, bt), :] = nxr
        xi_scr[pl.ds(off, bt), :] = nxi
        return nxr, nxi

    xr, xi = lax.fori_loop(0, tl, step, (cr_scr[...], ci_scr[...]))
    cr_scr[...] = xr
    ci_scr[...] = xi
    yc_tm = _dot(xr_scr[...].astype(BF16), cre_ref[...]) - _dot(xi_scr[...].astype(BF16), cim_ref[...])
    hi, lo = _split(yc_tm)
    yc = _dot(permt_ref[...], hi) + _dot(permt_ref[...], lo)
    y = jax.nn.gelu(yc + d_ref[...] * u)
    y = y * jax.nn.sigmoid(_dot(y.astype(BF16), wglu_ref[...]) + bglu_ref[...])
    y_ref[...] = y.reshape(bt, tl, S5_COLS)

    @pl.when(j == pl.num_programs(1) - 1)
    def _():
        xr_out[...] = xr
        xi_out[...] = xi


def _s5(u, perm, permt, br_bd, bi_bd, are_row, aim_row, ldt_row, cre_bd, cim_bd, d_row, wglu, bglu_row, x0r, x0i, bt, tl):
    b, l, _ = u.shape
    n = bt * tl
    const = lambda a, bb: pl.BlockSpec((a, bb), lambda i, j: (0, 0))
    state = pl.BlockSpec((bt, S5_STATE), lambda i, j: (i, 0))
    return pl.pallas_call(
        functools.partial(_s5_kernel, bt=bt, tl=tl),
        out_shape=[jax.ShapeDtypeStruct((b, l, S5_COLS), F32),
                   jax.ShapeDtypeStruct((b, S5_STATE), F32), jax.ShapeDtypeStruct((b, S5_STATE), F32)],
        grid=(b // bt, l // tl),
        in_specs=[pl.BlockSpec((bt, tl, S5_COLS), lambda i, j: (i, j, 0)),
                  const(n, n), const(n, n), const(S5_COLS, S5_STATE), const(S5_COLS, S5_STATE),
                  const(1, S5_STATE), const(1, S5_STATE), const(1, S5_STATE),
                  const(S5_STATE, S5_COLS), const(S5_STATE, S5_COLS), const(1, S5_COLS),
                  const(S5_COLS, S5_COLS), const(1, S5_COLS), state, state],
        out_specs=[pl.BlockSpec((bt, tl, S5_COLS), lambda i, j: (i, j, 0)), state, state],
        scratch_shapes=[pltpu.VMEM((n, S5_STATE), F32), pltpu.VMEM((n, S5_STATE), F32),
                        pltpu.VMEM((n, S5_STATE), F32), pltpu.VMEM((n, S5_STATE), F32),
                        pltpu.VMEM((bt, S5_STATE), F32), pltpu.VMEM((bt, S5_STATE), F32)],
        compiler_params=_cparams("parallel", "arbitrary"),
        name="s5_scan",
    )(u, perm, permt, br_bd, bi_bd, are_row, aim_row, ldt_row, cre_bd, cim_bd, d_row, wglu, bglu_row, x0r, x0i)


def _alibi_rows(rows, t):
    head = lax.broadcasted_iota(jnp.int32, (rows, 1), 0) // t
    slope = jnp.zeros((rows, 1), F32)
    for h in range(NSA_H):
        slope = jnp.where(head == h, 2.0 ** (-8.0 * (h + 1) / NSA_H), slope)
    return slope


def _stack_heads(x, width):
    return jnp.concatenate([x[:, h * width:(h + 1) * width] for h in range(NSA_H)], axis=0)


def _nsa_query_ext(q, slope, sel_bias, tile_bias):
    rows = q.shape[0]
    lane = lax.broadcasted_iota(jnp.int32, (rows, NSA_XW - NSA_X_BLOCKS), 1) + NSA_X_BLOCKS
    tail = jnp.where(lane == NSA_X_HI, float(V7X_LANES) * slope, jnp.where(lane == NSA_X_LO, slope, 0.0))
    if tile_bias is not None:
        tail = jnp.where(lane == NSA_X_ONE, tile_bias, tail)
    if sel_bias is None:
        sel_bias = jnp.zeros((rows, NSA_X_BLOCKS), F32)
    return jnp.concatenate([q, sel_bias, tail], axis=-1).astype(BF16)


def _nsa_compressed(s, mask, pv, t):
    s = jnp.where(mask, s, NEG)
    m = jnp.max(s, axis=-1, keepdims=True)
    e = jnp.where(mask, jnp.exp(s - m), 0.0)
    p = e / jnp.maximum(jnp.sum(e, axis=-1, keepdims=True), TINY)
    psum = p[0:t]
    for h in range(1, NSA_H):
        psum = psum + p[h * t:(h + 1) * t]
    return pv(p.astype(BF16)), psum


def _nsa_rank_select(imp, blk, qpos, n_top, axis):
    imp = jnp.where(blk == qpos // NSA_SEL, jnp.inf, jnp.where(blk * NSA_SEL <= qpos, imp, -jnp.inf))
    rank = jnp.zeros(imp.shape, jnp.int32)
    for c in range(imp.shape[axis]):
        other = imp[c:c + 1, :] if axis == 0 else imp[:, c:c + 1]
        rank = rank + jnp.where(other > imp, 1, jnp.where(other == imp, jnp.where(blk > c, 1, 0), 0))
    return jnp.where(rank < n_top, 1.0, 0.0)


def _nsa_combine(gates, o_c, o_s, o_w, t):
    outs = []
    for h in range(NSA_H):
        lane = PC_GATE_LANE + 3 * h
        rs = slice(h * t, (h + 1) * t)
        outs.append(gates[:, lane:lane + 1] * o_c[rs] + gates[:, lane + 1:lane + 2] * o_s[rs]
                    + gates[:, lane + 2:lane + 3] * o_w[rs])
    return jnp.concatenate(outs, axis=-1)


def _nsa_prompt_kernel(q_ref, rows_ref, win_ref, gate_ref, pairt_ref, kext_ref, eye_ref, o_ref,
                       cm_scr, ks_scr, kw_scr, vs_scr, vw_scr, *, tq, tk, l):
    i = pl.program_id(1)
    ncmp = l // NSA_CMP
    nsel = l // NSA_SEL
    d = NSA_D

    @pl.when(i == 0)
    def _():
        kv = rows_ref[:, 0:2 * d]
        cm_scr[...] = jnp.sum(kv.reshape(ncmp, NSA_CMP, 2 * d), axis=1) * (1.0 / NSA_CMP)
        kext = kext_ref[...]
        ks_scr[...] = jnp.concatenate([rows_ref[:, 2 * d:3 * d].astype(BF16), kext], axis=-1)
        kw_scr[...] = jnp.concatenate([win_ref[:, 0:d].astype(BF16), kext], axis=-1)
        vs_scr[...] = jnp.concatenate([rows_ref[:, 3 * d:4 * d].astype(BF16), kext], axis=-1)
        vw_scr[...] = jnp.concatenate([win_ref[:, d:2 * d].astype(BF16), kext], axis=-1)

    rows = NSA_H * tq
    q = _stack_heads(q_ref[...], d)
    slope = _alibi_rows(rows, tq)
    qofs = lax.rem(lax.broadcasted_iota(jnp.int32, (rows, 1), 0), tq)
    qpos = i * tq + qofs

    cm = cm_scr[...]
    cend = lax.broadcasted_iota(jnp.int32, (rows, ncmp), 1) * NSA_CMP + (NSA_CMP - 1)
    dc = qpos - cend
    s_c = _dot_nt(q.astype(BF16), cm[:, 0:d].astype(BF16)) - slope * dc.astype(F32)
    vcm = cm[:, d:2 * d].astype(BF16)
    o_c, psum = _nsa_compressed(s_c, dc >= 0, lambda p: _dot(p, vcm), tq)

    hi, lo = _split(psum)
    imp_t = _dot_nt(pairt_ref[...], hi) + _dot_nt(pairt_ref[...], lo)
    blk_t = lax.broadcasted_iota(jnp.int32, (nsel, tq), 0)
    qpos_t = i * tq + lax.broadcasted_iota(jnp.int32, (nsel, tq), 1)
    sel_t = _nsa_rank_select(imp_t, blk_t, qpos_t, min(NSA_TOPN, nsel), 0)
    sel = _dot_tn(sel_t.astype(BF16), eye_ref[...])
    sel_bias = jnp.concatenate([(sel - 1.0) * BIAS_BIG] * NSA_H, axis=0)

    q_sel = _nsa_query_ext(q, slope, sel_bias, None)
    acc0 = (jnp.full((rows, 1), NEG, F32), jnp.zeros((rows, d + NSA_XW), F32))

    def sel_chunk(c, carry, mask):
        start = pl.multiple_of(c * tk, tk)
        s = _dot_nt(q_sel, ks_scr[pl.ds(start, tk), :])
        vblk = vs_scr[pl.ds(start, tk), :]
        return _softmax_step_sum(s, mask, *carry, lambda p: _dot(p, vblk))

    diag = (i * tq) // tk
    carry = lax.fori_loop(0, diag, lambda c, carry: sel_chunk(c, carry, None), acc0)
    kofs = lax.broadcasted_iota(jnp.int32, (rows, tk), 1)
    m, acc = sel_chunk(diag, carry, kofs <= (i * tq - diag * tk) + qofs)
    o_s = acc[:, 0:d] / jnp.maximum(acc[:, d + NSA_X_ONE:d + NSA_X_ONE + 1], TINY)

    q_win = _nsa_query_ext(q, slope, None, None)
    wb = NSA_WINDOW + tq
    wstart = jnp.maximum(i * tq - NSA_WINDOW, 0)
    wofs = lax.broadcasted_iota(jnp.int32, (rows, wb), 1) + (wstart - i * tq)
    s = _dot_nt(q_win, kw_scr[pl.ds(pl.multiple_of(wstart, tq), wb), :])
    vblk = vw_scr[pl.ds(pl.multiple_of(wstart, tq), wb), :]
    m, acc = _softmax_step_sum(s, (wofs <= qofs) & (wofs > qofs - NSA_WINDOW), *acc0, lambda p: _dot(p, vblk))
    o_w = acc[:, 0:d] / jnp.maximum(acc[:, d + NSA_X_ONE:d + NSA_X_ONE + 1], TINY)
    o_ref[...] = _nsa_combine(gate_ref[...], o_c, o_s, o_w, tq)


def _nsa_prompt(qn, rows, win, gates, pairt, kext, eye, tq, tk):
    b, l, _ = qn.shape
    d = NSA_D
    assert tk % tq == 0 and NSA_WINDOW % tq == 0 and l // NSA_SEL <= NSA_X_BLOCKS and l >= NSA_WINDOW + tq
    seq = lambda n: pl.BlockSpec((None, l, n), lambda bi, i: (bi, 0, 0))
    blk = lambda n: pl.BlockSpec((None, tq, n), lambda bi, i: (bi, i, 0))
    const = lambda a: pl.BlockSpec(a.shape, lambda bi, i: (0, 0))
    return pl.pallas_call(
        functools.partial(_nsa_prompt_kernel, tq=tq, tk=tk, l=l),
        out_shape=jax.ShapeDtypeStruct((b, l, NSA_H * d), F32),
        grid=(b, l // tq),
        in_specs=[blk(NSA_H * d), seq(4 * d), seq(2 * d), blk(V7X_LANES), const(pairt), const(kext), const(eye)],
        out_specs=blk(NSA_H * d),
        scratch_shapes=[pltpu.VMEM((l // NSA_CMP, 2 * d), F32),
                        pltpu.VMEM((l, d + NSA_XW), BF16), pltpu.VMEM((l, d + NSA_XW), BF16),
                        pltpu.VMEM((l, d + NSA_XW), BF16), pltpu.VMEM((l, d + NSA_XW), BF16)],
        compiler_params=_cparams("parallel", "arbitrary"),
        name="nsa_prompt_attention",
    )(qn, rows, win, gates, pairt, kext, eye)


TPAD = V7X_SUBLANES


class _PagePipe:
    def __init__(self, cache_ref, pt_ref, buf, sem, layer, row0, feat, npg, nbt, n_steps, n_groups):
        self.cache_ref, self.pt_ref, self.buf, self.sem = cache_ref, pt_ref, buf, sem
        self.layer, self.row0, self.feat = layer, row0, feat
        self.npg, self.nbt, self.n_steps, self.n_groups = npg, nbt, n_steps, n_groups

    def _copies(self, g, step, slot):
        out = []
        for bi in range(self.nbt):
            for jj in range(self.npg):
                phys = self.pt_ref[g * self.nbt + bi, step * self.npg + jj]
                src = self.cache_ref.at[self.layer, phys, pl.ds(self.row0, self.feat), :]
                out.append(pltpu.make_async_copy(src, self.buf.at[slot, bi * self.npg + jj], self.sem.at[slot]))
        return out

    def start(self, g, step, slot):
        for c in self._copies(g, step, slot):
            c.start()

    def wait(self, g, step, slot):
        for c in self._copies(g, step, slot):
            c.wait()

    def run(self, init, body, finish):
        n_steps, n_groups = self.n_steps, self.n_groups
        self.start(0, 0, 0)

        def group(g, _):
            def step(k, carry):
                slot = lax.rem(g * n_steps + k, 2)
                self.start(g, k + 1, 1 - slot)
                self.wait(g, k, slot)
                return body(g, k, slot, carry)

            carry = lax.fori_loop(0, n_steps - 1, step, init(g))
            slot = lax.rem(g * n_steps + n_steps - 1, 2)
            self.start(jnp.minimum(g + 1, n_groups - 1), 0, 1 - slot)
            self.wait(g, n_steps - 1, slot)
            finish(g, body(g, n_steps - 1, slot, carry))
            return 0

        lax.fori_loop(0, n_groups, group, 0)
        self.wait(n_groups - 1, 0, (n_groups * n_steps) % 2)


def _mla_decode_kernel(pt_ref, qlat_ref, qrope_ref, knew_ref, cache_ref, o_ref, buf, sem, *, npg, nbt, layer, l_new):
    nb, n_pages = pt_ref.shape
    page = buf.shape[-1]
    rows = MLA_H * TPAD
    pipe = _PagePipe(cache_ref, pt_ref, buf, sem, layer, 0, MLA_ROW, npg, nbt, n_pages // npg, nb // nbt)
    tq = lax.rem(lax.broadcasted_iota(jnp.int32, (rows, TPAD), 0), TPAD)
    tk = lax.broadcasted_iota(jnp.int32, (rows, TPAD), 1)
    new_mask = (tk <= tq) & (tk < l_new)

    def queries(g):
        return [_stack_mla_q(qlat_ref[g * nbt + bi], qrope_ref[g * nbt + bi]) for bi in range(nbt)]

    def init(g):
        return tuple(_softmax_init(rows, MLA_KVLORA) for _ in range(nbt))

    def body(g, k, slot, carry):
        qs = queries(g)
        new = []
        for bi in range(nbt):
            pages = [buf[slot, bi * npg + jj].astype(BF16) for jj in range(npg)]
            s = jnp.concatenate([_dot(qs[bi], pg) for pg in pages], axis=-1)

            def pv(p, pages=pages):
                out = _dot_nt(p[:, 0:page], pages[0][0:MLA_KVLORA, :])
                for jj in range(1, npg):
                    out = out + _dot_nt(p[:, jj * page:(jj + 1) * page], pages[jj][0:MLA_KVLORA, :])
                return out

            new.append(_softmax_step(s, None, *carry[bi], pv))
        return tuple(new)

    def finish(g, carry):
        qs = queries(g)
        for bi in range(nbt):
            knew = knew_ref[g * nbt + bi].astype(BF16)
            mm, ll, aa = _softmax_step(_dot_nt(qs[bi], knew), new_mask, *carry[bi],
                                       lambda p, knew=knew: _dot(p, knew[:, :MLA_KVLORA]))
            o = aa / jnp.maximum(ll, TINY)
            o_ref[g * nbt + bi] = jnp.concatenate([o[h * TPAD:(h + 1) * TPAD] for h in range(MLA_H)], axis=-1)

    pipe.run(init, body, finish)


def _paged_call(kernel_fn, name, page_table, vmem_inputs, cache_t, out_shapes, feat, npg, nbt, extra_scratch=()):
    page = cache_t.shape[3]
    whole = lambda a: pl.BlockSpec(a.shape, lambda i, pt, n=len(a.shape): (0,) * n)
    grid_spec = pltpu.PrefetchScalarGridSpec(
        num_scalar_prefetch=1,
        grid=(1,),
        in_specs=[whole(a) for a in vmem_inputs] + [pl.BlockSpec(memory_space=pl.ANY)],
        out_specs=[whole(o) for o in out_shapes],
        scratch_shapes=[pltpu.VMEM((2, nbt * npg, feat, page), F32), pltpu.SemaphoreType.DMA((2,))]
                       + list(extra_scratch))
    return pl.pallas_call(kernel_fn, out_shape=out_shapes, grid_spec=grid_spec,
                          compiler_params=_cparams("arbitrary"), name=name)(page_table, *vmem_inputs, cache_t)


def _mla_decode(page_table, qlat, qrope, knew, cache_t, layer, npg, nbt, l_new):
    nb = page_table.shape[0]
    out = [jax.ShapeDtypeStruct((nb, TPAD, MLA_H * MLA_KVLORA), F32)]
    return _paged_call(functools.partial(_mla_decode_kernel, npg=npg, nbt=nbt, layer=layer, l_new=l_new),
                       "mla_paged_decode", page_table, [qlat, qrope, knew], cache_t, out, MLA_ROW, npg, nbt)[0]


def _nsa_cmp_decode_kernel(pt_ref, q_ref, pool_ref, pair_ref, cache_ref, oc_ref, sel_ref, buf, sem, cm_scr, *,
                           npg, nbt, layer, past_len):
    nb, n_pages = pt_ref.shape
    d = NSA_D
    n_steps = n_pages // npg
    rows = NSA_H * TPAD
    pipe = _PagePipe(cache_ref, pt_ref, buf, sem, layer, 0, 2 * d, npg, nbt, n_steps, nb // nbt)
    slope = _alibi_rows(rows, TPAD)
    qpos = past_len + lax.rem(lax.broadcasted_iota(jnp.int32, (rows, 1), 0), TPAD)
    ncmp = past_len // NSA_CMP
    nsel = past_len // NSA_SEL
    cend = lax.broadcasted_iota(jnp.int32, (rows, ncmp), 1) * NSA_CMP + (NSA_CMP - 1)
    dc = qpos - cend
    blk = lax.broadcasted_iota(jnp.int32, (TPAD, nsel), 1)

    def body(g, k, slot, carry):
        for bi in range(nbt):
            acc = jnp.zeros((2 * d, cm_scr.shape[3]), F32)
            for jj in range(npg):
                hi, lo = _split(buf[slot, bi * npg + jj])
                acc = acc + _dot(hi, pool_ref[jj]) + _dot(lo, pool_ref[jj])
            cm_scr[bi, k] = acc
        return carry

    def finish(g, carry):
        for bi in range(nbt):
            q = _stack_heads(q_ref[g * nbt + bi], d).astype(BF16)
            cm_t = jnp.concatenate([cm_scr[bi, kk] for kk in range(n_steps)], axis=-1)
            s_c = _dot(q, cm_t[0:d].astype(BF16)) - slope * dc.astype(F32)
            vcm_t = cm_t[d:2 * d].astype(BF16)
            o_c, psum = _nsa_compressed(s_c, dc >= 0, lambda p, vcm_t=vcm_t: _dot_nt(p, vcm_t), TPAD)
            oc_ref[g * nbt + bi] = o_c
            sel_ref[g * nbt + bi] = _nsa_rank_select(_dot_split(psum, pair_ref[...]), blk, qpos[0:TPAD],
                                                     NSA_TOPN - 1, 1)

    pipe.run(lambda g: 0, body, finish)


def _nsa_cmp_decode(page_table, qn, pool, pair, cache_t, layer, npg, nbt, past_len):
    nb, n_pages = page_table.shape
    d = NSA_D
    out = [jax.ShapeDtypeStruct((nb, NSA_H * TPAD, d), F32), jax.ShapeDtypeStruct((nb, TPAD, past_len // NSA_SEL), F32)]
    cm = pltpu.VMEM((nbt, n_pages // npg, 2 * d, npg * cache_t.shape[3] // NSA_CMP), F32)
    return _paged_call(functools.partial(_nsa_cmp_decode_kernel, npg=npg, nbt=nbt, layer=layer, past_len=past_len),
                       "nsa_compressed_decode", page_table, [qn, pool, pair], cache_t, out, 2 * d, npg, nbt, [cm])


def _nsa_sel_decode_kernel(pt_ref, q_ref, sel_ref, kext_ref, cache_ref, m_ref, l_ref, acc_ref, buf, sem, *,
                           npg, nbt, layer):
    nb, n_pages = pt_ref.shape
    page = buf.shape[-1]
    d = NSA_D
    rows = NSA_H * TPAD
    pipe = _PagePipe(cache_ref, pt_ref, buf, sem, layer, 2 * d, 2 * d, npg, nbt, n_pages // npg, nb // nbt)
    slope = _alibi_rows(rows, TPAD)

    def body(g, k, slot, carry):
        tile_bias = slope * jnp.asarray(k * (npg * page), F32)
        new = []
        for bi in range(nbt):
            b = g * nbt + bi
            sel_bias = jnp.concatenate([(sel_ref[b, k] - 1.0) * BIAS_BIG] * NSA_H, axis=0)
            q_ext = _nsa_query_ext(_stack_heads(q_ref[b], d), slope, sel_bias, tile_bias)
            pages = [buf[slot, bi * npg + jj].astype(BF16) for jj in range(npg)]
            s = jnp.concatenate([_dot(q_ext, jnp.concatenate([pages[jj][0:d, :], kext_ref[jj]], axis=0))
                                 for jj in range(npg)], axis=-1)

            def pv(p, pages=pages):
                out = _dot_nt(p[:, 0:page], pages[0][d:2 * d, :])
                for jj in range(1, npg):
                    out = out + _dot_nt(p[:, jj * page:(jj + 1) * page], pages[jj][d:2 * d, :])
                return out

            new.append(_softmax_step(s, None, *carry[bi], pv))
        return tuple(new)

    def finish(g, carry):
        for bi in range(nbt):
            m_ref[g * nbt + bi], l_ref[g * nbt + bi], acc_ref[g * nbt + bi] = carry[bi]

    pipe.run(lambda g: tuple(_softmax_init(rows, d) for _ in range(nbt)), body, finish)


def _nsa_finish_decode_kernel(q_ref, m_ref, l_ref, acc_ref, oc_ref, rnew_ref, wpast_ref, wnew_ref, gate_ref, o_ref, *,
                              nbt, past_len, l_new):
    d = NSA_D
    rows = NSA_H * TPAD
    slope = _alibi_rows(rows, TPAD)
    tq = lax.rem(lax.broadcasted_iota(jnp.int32, (rows, 1), 0), TPAD)
    tk = lax.broadcasted_iota(jnp.int32, (rows, TPAD), 1)
    new_mask = (tk <= tq) & (tk < l_new)
    new_bias = slope * (tq - tk).astype(F32)
    for bi in range(nbt):
        q = _stack_heads(q_ref[bi], d).astype(BF16)
        rnew = rnew_ref[bi].astype(BF16)
        sn = _dot_nt(q, rnew[:, 2 * d:3 * d]) + slope * (past_len + tk).astype(F32)
        mm, ll, aa = _softmax_step(sn, new_mask, m_ref[bi], l_ref[bi], acc_ref[bi],
                                   lambda p, rnew=rnew: _dot(p, rnew[:, 3 * d:4 * d]))
        o_s = aa / jnp.maximum(ll, TINY)
        wpast = wpast_ref[bi].astype(BF16)
        wlen = wpast.shape[1]
        dw = tq + (wlen - lax.broadcasted_iota(jnp.int32, (rows, wlen), 1))
        sw = _dot(q, wpast[0:d, :]) - slope * dw.astype(F32)
        mm, ll, aa = _softmax_step(sw, dw < NSA_WINDOW, *_softmax_init(rows, d),
                                   lambda p, wpast=wpast: _dot_nt(p, wpast[d:2 * d, :]))
        wnew = wnew_ref[bi].astype(BF16)
        swn = _dot_nt(q, wnew[:, 0:d]) - new_bias
        mm, ll, aa = _softmax_step(swn, new_mask, mm, ll, aa, lambda p, wnew=wnew: _dot(p, wnew[:, d:2 * d]))
        o_w = aa / jnp.maximum(ll, TINY)
        o_ref[bi] = _nsa_combine(gate_ref[bi], oc_ref[bi], o_s, o_w, TPAD)


def _nsa_sel_decode(page_table, qn, sel_steps, kext, o_c, rows_new, win_past_t, win_new, gates, cache_t, layer, npg,
                    nbt, past_len, l_new):
    nb = page_table.shape[0]
    d = NSA_D
    rows = NSA_H * TPAD
    wlen = win_past_t.shape[2]
    stats = [jax.ShapeDtypeStruct((nb, rows, 1), F32), jax.ShapeDtypeStruct((nb, rows, 1), F32),
             jax.ShapeDtypeStruct((nb, rows, d), F32)]
    m, l, acc = _paged_call(functools.partial(_nsa_sel_decode_kernel, npg=npg, nbt=nbt, layer=layer),
                            "nsa_selected_decode", page_table, [qn, sel_steps, kext], cache_t, stats, 2 * d, npg, nbt)
    per_b = lambda a, n: pl.BlockSpec((nbt, a, n), lambda b: (b, 0, 0))
    return pl.pallas_call(
        functools.partial(_nsa_finish_decode_kernel, nbt=nbt, past_len=past_len, l_new=l_new),
        out_shape=jax.ShapeDtypeStruct((nb, TPAD, NSA_H * d), F32),
        grid=(nb // nbt,),
        in_specs=[per_b(TPAD, NSA_H * d), per_b(rows, 1), per_b(rows, 1), per_b(rows, d), per_b(rows, d),
                  per_b(TPAD, 4 * d), per_b(2 * d, wlen), per_b(TPAD, 2 * d), per_b(TPAD, V7X_LANES)],
        out_specs=per_b(TPAD, NSA_H * d),
        compiler_params=_cparams("parallel"),
        name="nsa_window_finish_decode",
    )(qn, m, l, acc, o_c, rows_new, win_past_t, win_new, gates)


def _out_proj_kernel(x_ref, olat_ref, ohg_ref, onsa_ref, os5_ref, wuv_ref, wout_ref, g1_ref, sc2_ref, sh2_ref,
                     gffn_ref, x1_ref, h2_ref):
    w = NSA_H * NSA_D
    omla = _dot(olat_ref[...].astype(BF16), wuv_ref[...])
    mix = (_dot(omla.astype(BF16), wout_ref[0:w, :]) + _dot(ohg_ref[...].astype(BF16), wout_ref[w:2 * w, :])
           + _dot(onsa_ref[...].astype(BF16), wout_ref[2 * w:3 * w, :])
           + _dot(os5_ref[...].astype(BF16), wout_ref[3 * w:4 * w, :]))
    x1 = x_ref[...] + g1_ref[...] * mix
    x1_ref[...] = x1
    var = jnp.mean(x1 * x1, axis=-1, keepdims=True)
    h2 = (x1 * lax.rsqrt(var + EPS) * gffn_ref[...]) * (1.0 + sc2_ref[...]) + sh2_ref[...]
    h2_ref[...] = h2.astype(BF16)


def _out_proj(x2d, olat, ohg, onsa, os5, wuv_bd, wout, g1, sc2, sh2, g_ffn, rows_per_seq, tm):
    r, d = x2d.shape
    if rows_per_seq is not None:
        tps = rows_per_seq // tm
        mod_spec = pl.BlockSpec((None, 1, d), lambda i: (i // tps, 0, 0))
    else:
        mod_spec = pl.BlockSpec((tm, d), lambda i: (i, 0))
    row = lambda n: pl.BlockSpec((tm, n), lambda i: (i, 0))
    const = lambda a, b: pl.BlockSpec((a, b), lambda i: (0, 0))
    w = NSA_H * NSA_D
    return pl.pallas_call(
        _out_proj_kernel,
        out_shape=[jax.ShapeDtypeStruct((r, d), F32), jax.ShapeDtypeStruct((r, d), BF16)],
        grid=(r // tm,),
        in_specs=[row(d), row(MLA_H * MLA_KVLORA), row(w), row(w), row(w), const(MLA_H * MLA_KVLORA, w),
                  const(4 * w, d), mod_spec, mod_spec, mod_spec, const(1, d)],
        out_specs=[row(d), row(d)],
        compiler_params=_cparams("parallel"),
        name="out_proj_residual",
    )(x2d, olat, ohg, onsa, os5, wuv_bd, wout, g1, sc2, sh2, g_ffn)


def _ffn_tail(j, upg_scr, upv_scr, offs, tm, cwg_ref, cwv_ref, cbg_ref, cbv_ref, wd_ref, x1_ref, g2_ref, o_ref,
              acc_scr):
    g = cbg_ref[...]
    v = cbv_ref[...]
    for k in range(CONV_W):
        g = g + upg_scr[pl.ds(offs[k], tm), :] * cwg_ref[k:k + 1, :]
        v = v + upv_scr[pl.ds(offs[k], tm), :] * cwv_ref[k:k + 1, :]
    part = _dot((_silu(g) * v).astype(BF16), wd_ref[...])

    @pl.when(j == 0)
    def _():
        acc_scr[...] = part

    @pl.when(j > 0)
    def _():
        acc_scr[...] = acc_scr[...] + part

    @pl.when(j == pl.num_programs(1) - 1)
    def _():
        o_ref[...] = x1_ref[...] + g2_ref[...] * acc_scr[...]


def _ffn_seq_kernel(h2_ref, halo_ref, x1_ref, g2_ref, wg_ref, wv_ref, cwg_ref, cwv_ref, cbg_ref, cbv_ref, wd_ref,
                    o_ref, sg_ref, sv_ref, upg_scr, upv_scr, acc_scr, *, tm, tiles_per_seq, halo):
    i, j = pl.program_id(0), pl.program_id(1)
    he = jnp.concatenate([halo_ref[...], h2_ref[...]], axis=0)
    rowid = lax.broadcasted_iota(jnp.int32, (halo + tm, 1), 0)
    keep = jnp.logical_or(rowid >= halo, lax.rem(i, tiles_per_seq) != 0)
    upg_scr[...] = jnp.where(keep, _dot(he, wg_ref[...]), 0.0)
    upv_scr[...] = jnp.where(keep, _dot(he, wv_ref[...]), 0.0)
    sg_ref[...] = upg_scr[pl.ds(halo + tm - (CONV_W - 1), CONV_W - 1), :]
    sv_ref[...] = upv_scr[pl.ds(halo + tm - (CONV_W - 1), CONV_W - 1), :]
    offs = tuple(halo - (CONV_W - 1) + k for k in range(CONV_W))
    _ffn_tail(j, upg_scr, upv_scr, offs, tm, cwg_ref, cwv_ref, cbg_ref, cbv_ref, wd_ref, x1_ref, g2_ref, o_ref,
              acc_scr)


def _ffn_seq(h2, x1, g2, wg, wv, cwg, cwv, cbg, cbv, wd, rows_per_seq, tm, ck):
    r, d = x1.shape
    dff = wd.shape[0]
    nj = dff // ck
    tps = rows_per_seq // tm
    halo = 16
    hb = tm // halo
    xo, sg, sv = pl.pallas_call(
        functools.partial(_ffn_seq_kernel, tm=tm, tiles_per_seq=tps, halo=halo),
        out_shape=[jax.ShapeDtypeStruct((r, d), F32),
                   jax.ShapeDtypeStruct((r // tm, CONV_W - 1, dff), F32),
                   jax.ShapeDtypeStruct((r // tm, CONV_W - 1, dff), F32)],
        grid=(r // tm, nj),
        in_specs=[pl.BlockSpec((tm, d), lambda i, j: (i, 0)),
                  pl.BlockSpec((halo, d), lambda i, j: (jnp.maximum(i * hb - 1, 0), 0)),
                  pl.BlockSpec((tm, d), lambda i, j: (i, 0)),
                  pl.BlockSpec((None, 1, d), lambda i, j: (i // tps, 0, 0)),
                  pl.BlockSpec((d, ck), lambda i, j: (0, j)),
                  pl.BlockSpec((d, ck), lambda i, j: (0, j)),
                  pl.BlockSpec((CONV_W, ck), lambda i, j: (0, j)),
                  pl.BlockSpec((CONV_W, ck), lambda i, j: (0, j)),
                  pl.BlockSpec((1, ck), lambda i, j: (0, j)),
                  pl.BlockSpec((1, ck), lambda i, j: (0, j)),
                  pl.BlockSpec((ck, d), lambda i, j: (j, 0))],
        out_specs=[pl.BlockSpec((tm, d), lambda i, j: (i, 0)),
                   pl.BlockSpec((None, CONV_W - 1, ck), lambda i, j: (i, 0, j)),
                   pl.BlockSpec((None, CONV_W - 1, ck), lambda i, j: (i, 0, j))],
        scratch_shapes=[pltpu.VMEM((halo + tm, ck), F32), pltpu.VMEM((halo + tm, ck), F32),
                        pltpu.VMEM((tm, d), F32)],
        compiler_params=_cparams("parallel", "arbitrary"),
        name="ffn_conv_prompt",
    )(h2, h2, x1, g2, wg, wv, cwg, cwv, cbg, cbv, wd)
    return xo, sg[tps - 1::tps], sv[tps - 1::tps]


def _ffn_step_kernel(h2_ref, sg_in_ref, sv_in_ref, x1_ref, g2_ref, wg_ref, wv_ref, cwg_ref, cwv_ref, cbg_ref, cbv_ref,
                     wd_ref, o_ref, sg_ref, sv_ref, upg_scr, upv_scr, acc_scr, *, tm, nb):
    j = pl.program_id(1)
    hist = (CONV_W - 1) * nb
    upg_scr[0:hist, :] = sg_in_ref[...]
    upv_scr[0:hist, :] = sv_in_ref[...]
    upg_scr[hist:hist + tm, :] = _dot(h2_ref[...], wg_ref[...])
    upv_scr[hist:hist + tm, :] = _dot(h2_ref[...], wv_ref[...])
    sg_ref[...] = upg_scr[tm:tm + hist, :]
    sv_ref[...] = upv_scr[tm:tm + hist, :]
    offs = tuple(k * nb for k in range(CONV_W))
    _ffn_tail(j, upg_scr, upv_scr, offs, tm, cwg_ref, cwv_ref, cbg_ref, cbv_ref, wd_ref, x1_ref, g2_ref, o_ref,
              acc_scr)


def _ffn_step(h2, sg_in, sv_in, x1, g2, wg, wv, cwg, cwv, cbg, cbv, wd, nb, ck):
    tm, d = x1.shape
    dff = wd.shape[0]
    nj = dff // ck
    hist = (CONV_W - 1) * nb
    full = lambda a, b: pl.BlockSpec((a, b), lambda i, j: (0, 0))
    colblk = lambda a: pl.BlockSpec((a, ck), lambda i, j: (0, j))
    return pl.pallas_call(
        functools.partial(_ffn_step_kernel, tm=tm, nb=nb),
        out_shape=[jax.ShapeDtypeStruct((tm, d), F32), jax.ShapeDtypeStruct((hist, dff), F32),
                   jax.ShapeDtypeStruct((hist, dff), F32)],
        grid=(1, nj),
        in_specs=[full(tm, d), colblk(hist), colblk(hist), full(tm, d), full(tm, d), colblk(d), colblk(d),
                  colblk(CONV_W), colblk(CONV_W), colblk(1), colblk(1),
                  pl.BlockSpec((ck, d), lambda i, j: (j, 0))],
        out_specs=[full(tm, d), colblk(hist), colblk(hist)],
        scratch_shapes=[pltpu.VMEM((hist + tm, ck), F32), pltpu.VMEM((hist + tm, ck), F32),
                        pltpu.VMEM((tm, d), F32)],
        compiler_params=_cparams("parallel", "arbitrary"),
        name="ffn_conv_step",
    )(h2, sg_in, sv_in, x1, g2, wg, wv, cwg, cwv, cbg, cbv, wd)


def _final_norm_kernel(x_ref, g_ref, o_ref):
    x = x_ref[...]
    var = jnp.mean(x * x, axis=-1, keepdims=True)
    o_ref[...] = x * lax.rsqrt(var + EPS) * g_ref[...]


def _final_norm(x2d, g, tm):
    r, d = x2d.shape
    return pl.pallas_call(
        _final_norm_kernel,
        out_shape=jax.ShapeDtypeStruct((r, d), F32),
        grid=(r // tm,),
        in_specs=[pl.BlockSpec((tm, d), lambda i: (i, 0)), pl.BlockSpec((1, d), lambda i: (0, 0))],
        out_specs=pl.BlockSpec((tm, d), lambda i: (i, 0)),
        compiler_params=_cparams("parallel"),
        name="final_rmsnorm",
    )(x2d, g)


def _rope_tables(pos):
    half = MLA_ROPE // 2
    inv = ROPE_BASE ** (-jnp.arange(half, dtype=F32) / half)
    ang = pos.astype(F32)[:, None] * inv[None, :]
    c, s = jnp.cos(ang), jnp.sin(ang)
    return jnp.concatenate([c, c], axis=-1), jnp.concatenate([-s, s], axis=-1)


def _swap_halves(w):
    half = w.shape[-1] // 2
    return jnp.concatenate([w[..., half:], w[..., :half]], axis=-1)


def _block_diag(blocks):
    g, a, b = blocks.shape
    eye = jnp.eye(g, dtype=blocks.dtype)
    return (blocks[:, :, None, :] * eye[:, None, :, None]).reshape(g * a, g * b)


def _time_major_perm(bt, tl):
    n = bt * tl
    src = jnp.arange(n)
    dst = (src % tl) * bt + src // tl
    return (dst[:, None] == jnp.arange(n)[None, :]).astype(BF16).T


def _nsa_key_ext(pos, block):
    lane = jnp.arange(NSA_XW)[None, :]
    onehot = (lane == block[:, None]) & (lane < NSA_X_BLOCKS)
    ext = (onehot.astype(F32) + jnp.where(lane == NSA_X_HI, (pos // V7X_LANES)[:, None], 0)
           + jnp.where(lane == NSA_X_LO, (pos % V7X_LANES)[:, None], 0) + (lane == NSA_X_ONE))
    return ext.astype(BF16)


def _layer_consts(l, w_in, w_out, mla_w_uq, mla_w_uk, mla_w_uv, s5_b_re, s5_b_im, s5_c_re, s5_c_im, ffn_w_up,
                  ffn_w_down):
    d = w_in.shape[1]
    w = w_in[l]
    o = 0
    cq = w[:, o:o + MLA_QLORA]; o += MLA_QLORA
    ckv = w[:, o:o + MLA_KVLORA]; o += MLA_KVLORA
    kr = w[:, o:o + MLA_ROPE]; o += MLA_ROPE
    hgw = w[:, o:o + HG_COLS]; o += HG_COLS
    qn = w[:, o:o + NSA_H * NSA_D]; o += NSA_H * NSA_D
    kvs = w[:, o:o + 6 * NSA_D]; o += 6 * NSA_D
    gates = w[:, o:o + 3 * NSA_H]; o += 3 * NSA_H
    s5w = w[:, o:o + S5_COLS]
    pad = jnp.zeros((d, V7X_LANES - 2 * MLA_ROPE - 3 * NSA_H), w.dtype)
    w_perm = jnp.concatenate([cq, ckv, kr, _swap_halves(kr), gates, pad, hgw, qn, kvs, s5w], axis=1).astype(BF16)
    uq = mla_w_uq[l].reshape(MLA_QLORA, MLA_H, MLA_NOPE + MLA_ROPE)
    uq_rope = uq[:, :, MLA_NOPE:]
    wuq_perm = jnp.concatenate([uq[:, :, :MLA_NOPE].reshape(MLA_QLORA, -1), uq_rope.reshape(MLA_QLORA, -1),
                                _swap_halves(uq_rope).reshape(MLA_QLORA, -1)], axis=1).astype(BF16)
    return dict(
        w_perm=w_perm, wuq_perm=wuq_perm,
        wuk_bd=_block_diag(mla_w_uk[l].transpose(1, 2, 0)).astype(BF16),
        wuv_bd=_block_diag(mla_w_uv[l].transpose(1, 0, 2)).astype(BF16),
        wout=w_out[l].astype(BF16),
        br_bd=_block_diag(s5_b_re[l].transpose(0, 2, 1)).astype(BF16),
        bi_bd=_block_diag(s5_b_im[l].transpose(0, 2, 1)).astype(BF16),
        cre_bd=_block_diag(s5_c_re[l].transpose(0, 2, 1)).astype(BF16),
        cim_bd=_block_diag(s5_c_im[l].transpose(0, 2, 1)).astype(BF16),
        wg=ffn_w_up[l][:, :ffn_w_down.shape[1]].astype(BF16),
        wv=ffn_w_up[l][:, ffn_w_down.shape[1]:].astype(BF16),
        wd=ffn_w_down[l].astype(BF16),
    )


def _hgrn_state_to_bd(s):
    b = s.shape[0]
    blocks = s.transpose(0, 1, 3, 2)
    eye = jnp.eye(HG_H, dtype=s.dtype)
    return (blocks[:, :, :, None, :] * eye[None, :, None, :, None]).reshape(b, HG_H * HG_DV, HG_H * HG_DK)


def _hgrn_state_from_bd(st):
    b = st.shape[0]
    st5 = st.reshape(b, HG_H, HG_DV, HG_H, HG_DK)
    return jnp.stack([st5[:, h, :, h, :] for h in range(HG_H)], axis=1).transpose(0, 1, 3, 2)


def _pad_t(x):
    return jnp.pad(x, ((0, 0), (0, TPAD - x.shape[1]), (0, 0)))


def kernel(x_prompt, x_sample, c_prompt, c_sample, cache_mla, cache_nsa, cache_nsa_win, state_hgrn, state_s5_re, state_s5_im, state_ffn_conv, page_table, w_ada, b_ada, g_mix, g_ffn, w_in, w_out, mla_g_q, mla_g_kv, mla_w_uq, mla_w_uk, mla_w_uv, hg_lb, hg_g_out, s5_a_re, s5_a_im, s5_log_dt, s5_b_re, s5_b_im, s5_c_re, s5_c_im, s5_d, s5_w_glu, s5_b_glu, ffn_w_up, ffn_conv_w, ffn_conv_b, ffn_w_down, g_final):
    depth = w_in.shape[0]
    bp, lp, d = x_prompt.shape
    nb, ls, _ = x_sample.shape
    page = cache_mla.shape[2]
    n_pages = page_table.shape[1]
    past_len = n_pages * page
    dff = ffn_w_down.shape[1]
    rp, rs = bp * lp, nb * ls
    assert ls <= TPAD and page == V7X_LANES and lp % NSA_SEL == 0
    assert cache_nsa_win.shape[2] == NSA_WINDOW and past_len >= NSA_WINDOW

    tm_in, tm_row, tq_mla, tk_mla, tq_nsa, tk_nsa = 256, 512, 256, 256, 256, 512
    bt, tl_rec, c_hg = 8, 128, 16
    ck = 896
    npg, nbt = 16, 4
    npg_cmp, nbt_cmp = 32, 2
    assert (npg * page // NSA_SEL) == NSA_X_BLOCKS and npg_cmp * page // NSA_CMP == V7X_LANES

    ones_bd = _block_diag(jnp.ones((HG_H, HG_DK, HG_DV), BF16))
    cos_p, sin_p = _rope_tables(jnp.arange(lp))
    cos_s, sin_s = _rope_tables(past_len + jnp.arange(ls))
    cos_s, sin_s = jnp.tile(cos_s, (nb, 1)), jnp.tile(sin_s, (nb, 1))
    tile_h = lambda t: jnp.tile(t, (1, MLA_H))
    perm_p = _time_major_perm(bt, tl_rec)
    perm_s = _time_major_perm(bt, ls)
    ncmp_p, nsel_p = lp // NSA_CMP, lp // NSA_SEL
    cmp_per_sel = NSA_SEL // NSA_CMP
    pairt_p = (jnp.arange(nsel_p)[:, None] == jnp.arange(ncmp_p)[None, :] // cmp_per_sel).astype(BF16)
    kext_p = _nsa_key_ext(jnp.arange(lp), jnp.arange(lp) // NSA_SEL)
    eye_p = jnp.eye(nsel_p, NSA_X_BLOCKS, dtype=BF16)
    ncmp_s, nsel_s = past_len // NSA_CMP, past_len // NSA_SEL
    pair_s = (jnp.arange(ncmp_s)[:, None] // cmp_per_sel == jnp.arange(nsel_s)[None, :]).astype(BF16)
    tok = jnp.arange(npg * page)
    kext_s = _nsa_key_ext(tok, tok // NSA_SEL).reshape(npg, page, NSA_XW).transpose(0, 2, 1)
    ptok = jnp.arange(npg_cmp * page).reshape(npg_cmp, page)
    pool = ((ptok[:, :, None] // NSA_CMP == jnp.arange(V7X_LANES)[None, None, :]).astype(F32)
            * (1.0 / NSA_CMP)).astype(BF16)
    cache_mla_t = cache_mla.transpose(0, 1, 3, 2)
    cache_nsa_t = cache_nsa.transpose(0, 1, 3, 4, 2).reshape(cache_nsa.shape[0], cache_nsa.shape[1], 4 * NSA_D, page)
    win_t = cache_nsa_win.transpose(0, 1, 3, 4, 2).reshape(depth, nb, 2 * NSA_D, NSA_WINDOW)

    mod = _modulation(jnp.concatenate([c_prompt, c_sample], axis=0), w_ada, b_ada)
    xp = x_prompt.reshape(rp, d)
    xs = x_sample.reshape(rs, d)
    leaves_p, leaves_s = [], []
    row1 = lambda v: v.reshape(1, -1)
    for l in range(depth):
        cst = _layer_consts(l, w_in, w_out, mla_w_uq, mla_w_uk, mla_w_uv, s5_b_re, s5_b_im, s5_c_re, s5_c_im,
                            ffn_w_up, ffn_w_down)
        gout_row = row1(jnp.tile(hg_g_out[l], HG_H))
        are_row, aim_row = row1(s5_a_re[l]), row1(s5_a_im[l])
        ldt_row = row1(jnp.repeat(s5_log_dt[l], S5_N))
        d_row, bglu_row = row1(s5_d[l]), row1(s5_b_glu[l])
        wglu = s5_w_glu[l].astype(BF16)
        cwg, cwv = ffn_conv_w[l][:, :dff], ffn_conv_w[l][:, dff:]
        cbg, cbv = row1(ffn_conv_b[l][:dff]), row1(ffn_conv_b[l][dff:])
        s5_args = (cst["br_bd"], cst["bi_bd"], are_row, aim_row, ldt_row, cst["cre_bd"], cst["cim_bd"], d_row, wglu,
                   bglu_row)

        m6 = mod[l, :bp].reshape(bp, 6, 1, d)
        sh1, sc1, g1, sh2, sc2, g2 = (m6[:, i] for i in range(6))
        cq, mla_rows, hgc, qn, nsa_rows, win, gates, s5u = _proj_in(
            xp, sc1, sh1, row1(g_mix[l]), cst["w_perm"], row1(mla_g_kv[l]), cos_p, sin_p, lp, tm_in)
        qlat, qrope = _mla_q(cq, row1(mla_g_q[l]), cst["wuq_perm"], cst["wuk_bd"], tile_h(cos_p), tile_h(sin_p), lp,
                             tm_row)
        olat = _mla_prompt(qlat.reshape(bp, lp, -1), qrope.reshape(bp, lp, -1), mla_rows.reshape(bp, lp, -1),
                           tq_mla, tk_mla)
        ohg, st = _hgrn(hgc.reshape(bp, lp, -1), hg_lb, gout_row, ones_bd,
                        jnp.zeros((bp, HG_H * HG_DV, HG_H * HG_DK), F32), bt, tl_rec, c_hg, lp, l)
        onsa = _nsa_prompt(qn.reshape(bp, lp, -1), nsa_rows.reshape(bp, lp, -1), win.reshape(bp, lp, -1),
                           gates.reshape(bp, lp, -1), pairt_p, kext_p, eye_p, tq_nsa, tk_nsa)
        zero_state = jnp.zeros((bp, S5_STATE), F32)
        os5, s5r, s5i = _s5(s5u.reshape(bp, lp, -1), perm_p, perm_p.T, *s5_args, zero_state, zero_state, bt, tl_rec)
        x1, h2 = _out_proj(xp, olat.reshape(rp, -1), ohg.reshape(rp, -1), onsa.reshape(rp, -1), os5.reshape(rp, -1),
                           cst["wuv_bd"], cst["wout"], g1, sc2, sh2, row1(g_ffn[l]), lp, tm_row)
        xp, cg, cv = _ffn_seq(h2, x1, g2, cst["wg"], cst["wv"], cwg, cwv, cbg, cbv, cst["wd"], lp, tm_row, ck)
        wlen = min(NSA_WINDOW, lp)
        leaves_p.append(dict(
            mla=mla_rows.reshape(bp, lp, MLA_ROW), nsa=nsa_rows.reshape(bp, lp, 4, NSA_D),
            win=win.reshape(bp, lp, 2, NSA_D)[:, lp - wlen:], hgrn=_hgrn_state_from_bd(st),
            s5_re=s5r.reshape(bp, S5_G, S5_N), s5_im=s5i.reshape(bp, S5_G, S5_N),
            conv=jnp.concatenate([cg, cv], axis=-1)))

        m6 = jnp.repeat(mod[l, bp:].reshape(nb, 6, d), ls, axis=0)
        sh1, sc1, g1, sh2, sc2, g2 = (m6[:, i] for i in range(6))
        cq, mla_rows, hgc, qn, nsa_rows, win, gates, s5u = _proj_in(
            xs, sc1, sh1, row1(g_mix[l]), cst["w_perm"], row1(mla_g_kv[l]), cos_s, sin_s, None, rs)
        qlat, qrope = _mla_q(cq, row1(mla_g_q[l]), cst["wuq_perm"], cst["wuk_bd"], tile_h(cos_s), tile_h(sin_s),
                             None, rs)
        seq = lambda a: _pad_t(a.reshape(nb, ls, -1))
        olat = _mla_decode(page_table, seq(qlat), seq(qrope), seq(mla_rows), cache_mla_t, l, npg, nbt, ls)[:, :ls]
        ohg, st = _hgrn(seq(hgc), hg_lb, gout_row, ones_bd, _hgrn_state_to_bd(state_hgrn[l]), bt, TPAD, TPAD, ls, l)
        ohg = ohg[:, :ls]
        o_c, sel = _nsa_cmp_decode(page_table, seq(qn), pool, pair_s, cache_nsa_t, l, npg_cmp, nbt_cmp, past_len)
        sel_steps = sel.reshape(nb, TPAD, n_pages // npg, NSA_X_BLOCKS).transpose(0, 2, 1, 3)
        onsa = _nsa_sel_decode(page_table, seq(qn), sel_steps, kext_s, o_c, seq(nsa_rows), win_t[l], seq(win),
                               seq(gates), cache_nsa_t, l, npg, nbt, past_len, ls)[:, :ls]
        os5, s5r, s5i = _s5(s5u.reshape(nb, ls, -1), perm_s, perm_s.T, *s5_args,
                            state_s5_re[l].reshape(nb, S5_STATE), state_s5_im[l].reshape(nb, S5_STATE), bt, ls)
        x1, h2 = _out_proj(xs, olat.reshape(rs, -1), ohg.reshape(rs, -1), onsa.reshape(rs, -1), os5.reshape(rs, -1),
                           cst["wuv_bd"], cst["wout"], g1, sc2, sh2, row1(g_ffn[l]), None, rs)
        tmaj = lambda a: a.reshape(nb, ls, -1).transpose(1, 0, 2).reshape(rs, -1)
        conv_in = state_ffn_conv[l].transpose(1, 0, 2).reshape((CONV_W - 1) * nb, 2 * dff)
        xs_tm, sg, sv = _ffn_step(tmaj(h2), conv_in[:, :dff], conv_in[:, dff:], tmaj(x1), tmaj(g2), cst["wg"],
                                  cst["wv"], cwg, cwv, cbg, cbv, cst["wd"], nb, ck)
        xs = xs_tm.reshape(ls, nb, d).transpose(1, 0, 2).reshape(rs, d)
        conv_out = jnp.concatenate([sg, sv], axis=-1).reshape(CONV_W - 1, nb, 2 * dff).transpose(1, 0, 2)
        win_state = jnp.concatenate([cache_nsa_win[l], win.reshape(nb, ls, 2, NSA_D)], axis=1)[:, ls:]
        leaves_s.append(dict(
            mla=mla_rows.reshape(nb, ls, MLA_ROW), nsa=nsa_rows.reshape(nb, ls, 4, NSA_D), win=win_state,
            hgrn=_hgrn_state_from_bd(st), s5_re=s5r.reshape(nb, S5_G, S5_N), s5_im=s5i.reshape(nb, S5_G, S5_N),
            conv=conv_out))

    y_prompt = _final_norm(xp, row1(g_final), tm_row).reshape(bp, lp, d)
    y_sample = _final_norm(xs, row1(g_final), rs).reshape(nb, ls, d)
    stk = lambda lst, name: jnp.stack([dd[name] for dd in lst], axis=0)
    out = [y_prompt, y_sample]
    for name in ("mla", "nsa", "win", "hgrn", "s5_re", "s5_im", "conv"):
        out += [stk(leaves_p, name), stk(leaves_s, name)]
    return tuple(out)
```
